```python
import math
import jax
import jax.numpy as jnp
from jax import lax
import numpy as np

D_MODEL = 2048
BATCH = 1
SEQ = 16384
DEPTH = 2

PLE_DIM = 256
D_FF = 5632
D_SSM = 1024
SSM_GROUP = 16
N_GROUPS = D_SSM // SSM_GROUP
STATE = 64
N_HEADS = 8
HEAD_DIM = 64
QK_WIDTH = N_HEADS * 2 * HEAD_DIM
V_WIDTH = N_HEADS * 2 * HEAD_DIM
IN_WIDTH = D_SSM + 2 * QK_WIDTH + V_WIDTH + 2 * D_MODEL
ROT_DIM = HEAD_DIM // 4
ROPE_THETA = 500000.0
Q_BLOCK = 128
LN_EPS = 1e-5
RMS_EPS = 1e-5
NEG_INF = -1e30
ALPHA = (2 * DEPTH) ** 0.25
BETA = (8 * DEPTH) ** -0.25
DT_MIN = 0.001
DT_MAX = 0.1

kernel_name = "hybrid_s5_diffattn_macaron_deepnorm"


def _layer_norm(x, g, b):
    xf = x.astype(jnp.float32)
    mu = jnp.mean(xf, axis=-1, keepdims=True)
    xc = xf - mu
    var = jnp.mean(xc * xc, axis=-1, keepdims=True)
    y = xc * lax.rsqrt(var + LN_EPS) * g.astype(jnp.float32) + b.astype(jnp.float32)
    return y.astype(x.dtype)


def _swiglu(x, w_gate, w_up, w_down):
    return (jax.nn.silu(x @ w_gate) * (x @ w_up)) @ w_down


def _rope(t, positions):
    half = ROT_DIM // 2
    inv_freq = ROPE_THETA ** (-jnp.arange(0, ROT_DIM, 2, dtype=jnp.float32) / ROT_DIM)
    ang = positions.astype(jnp.float32)[:, :, None] * inv_freq
    cos = jnp.cos(ang)[:, :, None, None, :].astype(t.dtype)
    sin = jnp.sin(ang)[:, :, None, None, :].astype(t.dtype)
    t1 = t[..., :half]
    t2 = t[..., half:ROT_DIM]
    return jnp.concatenate([t1 * cos - t2 * sin, t2 * cos + t1 * sin, t[..., ROT_DIM:]], axis=-1)


def _s5_group(args):
    u, a_re, a_im, b_re, b_im, c_re, c_im, log_dt = args
    f32 = jnp.float32
    u = u.astype(f32)
    a_re, a_im = a_re.astype(f32), a_im.astype(f32)
    b_re, b_im = b_re.astype(f32), b_im.astype(f32)
    c_re, c_im = c_re.astype(f32), c_im.astype(f32)
    dt = jnp.exp(log_dt.astype(f32))
    mag = jnp.exp(a_re * dt)
    lb_re = mag * jnp.cos(a_im * dt)
    lb_im = mag * jnp.sin(a_im * dt)
    nr, ni = lb_re - 1.0, lb_im
    den = a_re * a_re + a_im * a_im
    coef_re = (nr * a_re + ni * a_im) / den
    coef_im = (ni * a_re - nr * a_im) / den
    bb_re = coef_re[:, None] * b_re - coef_im[:, None] * b_im
    bb_im = coef_re[:, None] * b_im + coef_im[:, None] * b_re
    bu_re = jnp.einsum('blh,ph->blp', u, bb_re)
    bu_im = jnp.einsum('blh,ph->blp', u, bb_im)
    ar = jnp.broadcast_to(lb_re, bu_re.shape)
    ai = jnp.broadcast_to(lb_im, bu_im.shape)

    def combine(e1, e2):
        ar1, ai1, br1, bi1 = e1
        ar2, ai2, br2, bi2 = e2
        return (ar2 * ar1 - ai2 * ai1,
                ar2 * ai1 + ai2 * ar1,
                ar2 * br1 - ai2 * bi1 + br2,
                ar2 * bi1 + ai2 * br1 + bi2)

    _, _, s_re, s_im = lax.associative_scan(combine, (ar, ai, bu_re, bu_im), axis=1)
    return jnp.einsum('blp,hp->blh', s_re, c_re) - jnp.einsum('blp,hp->blh', s_im, c_im)


def _s5_branch(u, a_re, a_im, b_re, b_im, c_re, c_im, log_dt, d_skip, w_glu):
    bsz, length, _ = u.shape
    ug = u.reshape(bsz, length, N_GROUPS, SSM_GROUP).transpose(2, 0, 1, 3)
    y = lax.map(_s5_group, (ug, a_re, a_im, b_re, b_im, c_re, c_im, log_dt))
    y = y.transpose(1, 2, 0, 3).reshape(bsz, length, D_SSM).astype(u.dtype)
    y = y + d_skip * u
    z = jax.nn.gelu(y, approximate=False)
    return z * jax.nn.sigmoid(z @ w_glu)


def _diff_attention(q, k, v, positions, lam, subln_g, lam_init):
    bsz, length = q.shape[0], q.shape[1]
    q = _rope(q, positions) * (HEAD_DIM ** -0.5)
    k = _rope(k, positions)
    q1 = q[:, :, :, 0].transpose(0, 2, 1, 3)
    q2 = q[:, :, :, 1].transpose(0, 2, 1, 3)
    k1 = k[:, :, :, 0].transpose(0, 2, 1, 3)
    k2 = k[:, :, :, 1].transpose(0, 2, 1, 3)
    vt = v.transpose(0, 2, 1, 3)
    n_blocks = length // Q_BLOCK
    key_pos = jnp.arange(length)

    def to_blocks(t):
        return t.reshape(bsz, N_HEADS, n_blocks, Q_BLOCK, HEAD_DIM).transpose(2, 0, 1, 3, 4)

    def block(args):
        bi, qa, qb = args
        q_pos = bi * Q_BLOCK + jnp.arange(Q_BLOCK)
        mask = key_pos[None, :] <= q_pos[:, None]
        s1 = jnp.einsum('bhqd,bhkd->bhqk', qa, k1).astype(jnp.float32)
        s2 = jnp.einsum('bhqd,bhkd->bhqk', qb, k2).astype(jnp.float32)
        p1 = jax.nn.softmax(jnp.where(mask, s1, NEG_INF), axis=-1)
        p2 = jax.nn.softmax(jnp.where(mask, s2, NEG_INF), axis=-1)
        w = (p1 - lam * p2).astype(vt.dtype)
        return jnp.einsum('bhqk,bhkd->bhqd', w, vt)

    out = lax.map(block, (jnp.arange(n_blocks), to_blocks(q1), to_blocks(q2)))
    out = out.transpose(1, 0, 3, 2, 4).reshape(bsz, length, N_HEADS, 2 * HEAD_DIM)
    of = out.astype(jnp.float32)
    of = of * lax.rsqrt(jnp.mean(of * of, axis=-1, keepdims=True) + RMS_EPS)
    of = of * subln_g.astype(jnp.float32) * (1.0 - lam_init)
    return of.reshape(bsz, length, N_HEADS * 2 * HEAD_DIM).astype(v.dtype)


def setup_inputs(seed: int = 0) -> dict:
    key = jax.random.key(seed)
    ks = iter(jax.random.split(key, 64))
    f32 = jnp.float32

    def nrm(shape, scale):
        return scale * jax.random.normal(next(ks), shape, f32)

    def gain(shape):
        return 1.0 + nrm(shape, 0.02)

    L = DEPTH
    d = D_MODEL
    inp = {}
    inp["x"] = nrm((BATCH, SEQ, d), 1.0)
    inp["p"] = nrm((DEPTH, BATCH, SEQ, PLE_DIM), 1.0)
    inp["positions"] = jnp.broadcast_to(jnp.arange(SEQ, dtype=jnp.int32), (BATCH, SEQ))
    inp["ffn1_w_gate"] = nrm((L, d, D_FF), d ** -0.5)
    inp["ffn1_w_up"] = nrm((L, d, D_FF), d ** -0.5)
    inp["ffn1_w_down"] = nrm((L, D_FF, d), BETA * D_FF ** -0.5)
    inp["ln1_g"] = gain((L, d))
    inp["ln1_b"] = nrm((L, d), 0.02)
    inp["w_in"] = nrm((L, d, IN_WIDTH), d ** -0.5)
    inp["ssm_a_re"] = -0.5 + nrm((L, N_GROUPS, STATE), 0.01)
    inp["ssm_a_im"] = math.pi * jnp.arange(STATE, dtype=f32)[None, None, :] + nrm((L, N_GROUPS, STATE), 0.01)
    inp["ssm_b_re"] = nrm((L, N_GROUPS, STATE, SSM_GROUP), (2.0 * SSM_GROUP) ** -0.5)
    inp["ssm_b_im"] = nrm((L, N_GROUPS, STATE, SSM_GROUP), (2.0 * SSM_GROUP) ** -0.5)
    inp["ssm_c_re"] = nrm((L, N_GROUPS, SSM_GROUP, STATE), (2.0 * STATE) ** -0.5)
    inp["ssm_c_im"] = nrm((L, N_GROUPS, SSM_GROUP, STATE), (2.0 * STATE) ** -0.5)
    inp["ssm_log_dt"] = jax.random.uniform(next(ks), (L, N_GROUPS), f32,
                                           minval=math.log(DT_MIN), maxval=math.log(DT_MAX))
    inp["ssm_d"] = nrm((L, D_SSM), 1.0)
    inp["ssm_w_glu"] = nrm((L, D_SSM, D_SSM), D_SSM ** -0.5)
    inp["w_branch_ssm"] = nrm((L, D_SSM, d), D_SSM ** -0.5)
    inp["lambda_q1"] = nrm((L, HEAD_DIM), 0.1)
    inp["lambda_k1"] = nrm((L, HEAD_DIM), 0.1)
    inp["lambda_q2"] = nrm((L, HEAD_DIM), 0.1)
    inp["lambda_k2"] = nrm((L, HEAD_DIM), 0.1)
    inp["attn_subln_g"] = gain((L, 2 * HEAD_DIM))
    inp["w_branch_attn"] = nrm((L, V_WIDTH, d), V_WIDTH ** -0.5)
    inp["w_out"] = nrm((L, d, d), BETA * d ** -0.5)
    inp["ln2_g"] = gain((L, d))
    inp["ln2_b"] = nrm((L, d), 0.02)
    inp["ffn2_w_gate"] = nrm((L, d, D_FF), d ** -0.5)
    inp["ffn2_w_up"] = nrm((L, d, D_FF), d ** -0.5)
    inp["ffn2_w_down"] = nrm((L, D_FF, d), BETA * D_FF ** -0.5)
    inp["ln3_g"] = gain((L, d))
    inp["ln3_b"] = nrm((L, d), 0.02)
    inp["ple_w_gate"] = nrm((L, d, d), d ** -0.5)
    inp["ple_w_proj"] = nrm((L, PLE_DIM, d), BETA * PLE_DIM ** -0.5)
    inp["ln4_g"] = gain((L, d))
    inp["ln4_b"] = nrm((L, d), 0.02)
    return inp


def reference(x, p, positions, ffn1_w_gate, ffn1_w_up, ffn1_w_down, ln1_g, ln1_b,
              w_in, ssm_a_re, ssm_a_im, ssm_b_re, ssm_b_im, ssm_c_re, ssm_c_im,
              ssm_log_dt, ssm_d, ssm_w_glu, w_branch_ssm,
              lambda_q1, lambda_k1, lambda_q2, lambda_k2, attn_subln_g, w_branch_attn,
              w_out, ln2_g, ln2_b, ffn2_w_gate, ffn2_w_up, ffn2_w_down, ln3_g, ln3_b,
              ple_w_gate, ple_w_proj, ln4_g, ln4_b):
    bsz, length, _ = x.shape
    splits = [D_SSM, D_SSM + QK_WIDTH, D_SSM + 2 * QK_WIDTH,
              D_SSM + 2 * QK_WIDTH + V_WIDTH, D_SSM + 2 * QK_WIDTH + V_WIDTH + D_MODEL]
    for i in range(DEPTH):
        lam_init = 0.8 - 0.6 * math.exp(-0.3 * i)
        x = _layer_norm(ALPHA * x + 0.5 * _swiglu(x, ffn1_w_gate[i], ffn1_w_up[i], ffn1_w_down[i]),
                        ln1_g[i], ln1_b[i])
        h = x @ w_in[i]
        u, q, k, v, g_a, g_b = jnp.split(h, splits, axis=-1)
        q = q.reshape(bsz, length, N_HEADS, 2, HEAD_DIM)
        k = k.reshape(bsz, length, N_HEADS, 2, HEAD_DIM)
        v = v.reshape(bsz, length, N_HEADS, 2 * HEAD_DIM)
        lam = (jnp.exp(jnp.sum(lambda_q1[i].astype(jnp.float32) * lambda_k1[i].astype(jnp.float32)))
               - jnp.exp(jnp.sum(lambda_q2[i].astype(jnp.float32) * lambda_k2[i].astype(jnp.float32)))
               + lam_init)
        y_a = _s5_branch(u, ssm_a_re[i], ssm_a_im[i], ssm_b_re[i], ssm_b_im[i],
                         ssm_c_re[i], ssm_c_im[i], ssm_log_dt[i], ssm_d[i], ssm_w_glu[i]) @ w_branch_ssm[i]
        y_b = _diff_attention(q, k, v, positions, lam, attn_subln_g[i], lam_init) @ w_branch_attn[i]
        mix = (jax.nn.sigmoid(g_a) * y_a + jax.nn.sigmoid(g_b) * y_b) @ w_out[i]
        x = _layer_norm(ALPHA * x + mix, ln2_g[i], ln2_b[i])
        x = _layer_norm(ALPHA * x + 0.5 * _swiglu(x, ffn2_w_gate[i], ffn2_w_up[i], ffn2_w_down[i]),
                        ln3_g[i], ln3_b[i])
        ple = jax.nn.sigmoid(x @ ple_w_gate[i]) * (p[i] @ ple_w_proj[i])
        x = _layer_norm(ALPHA * x + ple, ln4_g[i], ln4_b[i])
    return x
```

```python
import functools
import math

import jax
import jax.numpy as jnp
from jax import lax
from jax.experimental import pallas as pl
from jax.experimental.pallas import tpu as pltpu

F32 = jnp.float32
BF16 = jnp.bfloat16

DEPTH = 2
SSM_WIDTH = 1024
SSM_GROUP = 16
SSM_GROUPS = SSM_WIDTH // SSM_GROUP
SSM_STATE = 64
N_HEADS = 8
HEAD_DIM = 64
QK_WIDTH = N_HEADS * 2 * HEAD_DIM
V_WIDTH = N_HEADS * 2 * HEAD_DIM
ROT_DIM = HEAD_DIM // 4
ROPE_THETA = 500000.0
LN_EPS = 1e-5
RMS_EPS = 1e-5
NEG_INF = -1e30
ALPHA = (2 * DEPTH) ** 0.25

LANES = 128
VMEM_LIMIT_BYTES = 56 * 1024 * 1024

FFN_ROWS = 512
FFN_COLS = 512
PROJ_ROWS = 512
PROJ_COLS = 1024
SSM_CHUNK = 128
SSM_SLABS = 4
ATTN_Q = 512
ATTN_K = 512
MIX_ROWS = 256
PLE_ROWS = 512


def _sigmoid(x):
    return 1.0 / (1.0 + jnp.exp(-x))


def _layer_norm(y, g, b):
    mu = jnp.mean(y, axis=-1, keepdims=True)
    yc = y - mu
    var = jnp.mean(yc * yc, axis=-1, keepdims=True)
    return yc * lax.rsqrt(var + LN_EPS) * g + b


def _params(*semantics):
    return pltpu.CompilerParams(dimension_semantics=semantics, vmem_limit_bytes=VMEM_LIMIT_BYTES)


def _ffn_ln_kernel(x_ref, wg_ref, wu_ref, wd_ref, g_ref, b_ref, o_ref, xb_ref, acc_ref):
    j = pl.program_id(1)

    @pl.when(j == 0)
    def _():
        xb_ref[...] = x_ref[...].astype(BF16)

    xb = xb_ref[...]
    hg = jnp.dot(xb, wg_ref[...], preferred_element_type=F32)
    hu = jnp.dot(xb, wu_ref[...], preferred_element_type=F32)
    act = (hg * _sigmoid(hg) * hu).astype(BF16)
    part = jnp.dot(act, wd_ref[...], preferred_element_type=F32)

    @pl.when(j == 0)
    def _():
        acc_ref[...] = part

    @pl.when(j > 0)
    def _():
        acc_ref[...] += part

    @pl.when(j == pl.num_programs(1) - 1)
    def _():
        y = ALPHA * x_ref[...] + 0.5 * acc_ref[...]
        o_ref[...] = _layer_norm(y, g_ref[...], b_ref[...])


def _ffn_ln(x, wg, wu, wd, g, b):
    n, d = x.shape
    f = wg.shape[1]
    tm = min(FFN_ROWS, n)
    tf = FFN_COLS
    return pl.pallas_call(
        _ffn_ln_kernel,
        grid=(n // tm, f // tf),
        in_specs=[
            pl.BlockSpec((tm, d), lambda i, j: (i, 0)),
            pl.BlockSpec((d, tf), lambda i, j: (0, j)),
            pl.BlockSpec((d, tf), lambda i, j: (0, j)),
            pl.BlockSpec((tf, d), lambda i, j: (j, 0)),
            pl.BlockSpec((1, d), lambda i, j: (0, 0)),
            pl.BlockSpec((1, d), lambda i, j: (0, 0)),
        ],
        out_specs=pl.BlockSpec((tm, d), lambda i, j: (i, 0)),
        out_shape=jax.ShapeDtypeStruct((n, d), F32),
        scratch_shapes=[pltpu.VMEM((tm, d), BF16), pltpu.VMEM((tm, d), F32)],
        compiler_params=_params("parallel", "arbitrary"),
        name="ffn_ln",
    )(x, wg, wu, wd, g, b)


def _rotary(h, cos, sin_lo, sin_hi):
    half = ROT_DIM // 2
    return (h * cos + pltpu.roll(h, half, 1) * sin_hi
            + pltpu.roll(h, LANES - half, 1) * sin_lo)


def _inproj_kernel(x_ref, w_ref, cos_ref, slo_ref, shi_ref, u_ref, qk_ref, v_ref, g_ref, xb_ref):
    j = pl.program_id(1)

    @pl.when(j == 0)
    def _():
        xb_ref[...] = x_ref[...].astype(BF16)

    h = jnp.dot(xb_ref[...], w_ref[...], preferred_element_type=F32)

    @pl.when(j == 0)
    def _():
        u_ref[...] = h

    def rope_store(scale):
        cos, slo, shi = cos_ref[...], slo_ref[...], shi_ref[...]
        for c in range(h.shape[1] // LANES):
            hc = h[:, c * LANES:(c + 1) * LANES]
            qk_ref[:, c * LANES:(c + 1) * LANES] = (_rotary(hc, cos, slo, shi) * scale).astype(BF16)

    @pl.when(j == 1)
    def _():
        rope_store(HEAD_DIM ** -0.5)

    @pl.when(j == 2)
    def _():
        rope_store(1.0)

    @pl.when(j == 3)
    def _():
        v_ref[...] = h.astype(BF16)

    @pl.when(j >= 4)
    def _():
        g_ref[...] = _sigmoid(h).astype(BF16)


def _inproj(x, w, cos, slo, shi):
    n, d = x.shape
    tm = min(PROJ_ROWS, n)
    tn = PROJ_COLS
    width = w.shape[1]
    n_gate = (width - SSM_WIDTH - 2 * QK_WIDTH - V_WIDTH) // tn
    assert SSM_WIDTH == tn and QK_WIDTH == tn and V_WIDTH == tn
    tab = pl.BlockSpec((tm, LANES), lambda i, j: (i, 0))
    return pl.pallas_call(
        _inproj_kernel,
        grid=(n // tm, width // tn),
        in_specs=[
            pl.BlockSpec((tm, d), lambda i, j: (i, 0)),
            pl.BlockSpec((d, tn), lambda i, j: (0, j)),
            tab, tab, tab,
        ],
        out_specs=[
            pl.BlockSpec((tm, tn), lambda i, j: (i, 0)),
            pl.BlockSpec((tm, tn), lambda i, j: (i, jnp.clip(j - 1, 0, 1))),
            pl.BlockSpec((tm, tn), lambda i, j: (i, 0)),
            pl.BlockSpec((tm, tn), lambda i, j: (i, jnp.clip(j - 4, 0, n_gate - 1))),
        ],
        out_shape=[
            jax.ShapeDtypeStruct((n, SSM_WIDTH), F32),
            jax.ShapeDtypeStruct((n, 2 * QK_WIDTH), BF16),
            jax.ShapeDtypeStruct((n, V_WIDTH), BF16),
            jax.ShapeDtypeStruct((n, n_gate * tn), BF16),
        ],
        scratch_shapes=[pltpu.VMEM((tm, d), BF16)],
        compiler_params=_params("parallel", "arbitrary"),
        name="inproj",
    )(x, w, cos, slo, shi)


def _ssm_kernel(u_ref, bbd_ref, cbd_ref, pn_re_ref, pn_im_ref, pp_re_ref, pp_im_ref,
                lb_re_ref, lb_im_ref, d_ref, tri_ref, y_ref, car_re_ref, car_im_ref):
    @pl.when(pl.program_id(0) == 0)
    def _():
        car_re_ref[...] = jnp.zeros_like(car_re_ref)
        car_im_ref[...] = jnp.zeros_like(car_im_ref)

    t_len = u_ref.shape[0]
    slab_in = u_ref.shape[1] // SSM_SLABS
    slab_st = pn_re_ref.shape[1] // SSM_SLABS
    u = u_ref[...]
    ub = u.astype(BF16)
    tri = tri_ref[...]
    for s in range(SSM_SLABS):
        cin = slice(s * slab_in, (s + 1) * slab_in)
        cst = slice(s * slab_st, (s + 1) * slab_st)
        bu = jnp.dot(ub[:, cin], bbd_ref[s], preferred_element_type=F32)
        br, bi = bu[:, :slab_st], bu[:, slab_st:]
        nr, ni = pn_re_ref[:, cst], pn_im_ref[:, cst]
        z = jnp.concatenate([br * nr - bi * ni, br * ni + bi * nr], axis=1)
        z_hi = z.astype(BF16)
        z_lo = (z - z_hi.astype(F32)).astype(BF16)
        cs = (jnp.dot(tri, z_hi, preferred_element_type=F32)
              + jnp.dot(tri, z_lo, preferred_element_type=F32))
        lr, li = lb_re_ref[:, cst], lb_im_ref[:, cst]
        kr, ki = car_re_ref[:, cst], car_im_ref[:, cst]
        cr = cs[:, :slab_st] + (lr * kr - li * ki)
        ci = cs[:, slab_st:] + (lr * ki + li * kr)
        pr, pi = pp_re_ref[:, cst], pp_im_ref[:, cst]
        sr = cr * pr - ci * pi
        si = cr * pi + ci * pr
        car_re_ref[:, cst] = sr[t_len - 1:t_len, :]
        car_im_ref[:, cst] = si[t_len - 1:t_len, :]
        sb = jnp.concatenate([sr, si], axis=1).astype(BF16)
        y = jnp.dot(sb, cbd_ref[s], preferred_element_type=F32)
        y_ref[:, cin] = y + d_ref[:, cin] * u[:, cin]


def _ssm(u, bbd, cbd, pn_re, pn_im, pp_re, pp_im, lb_re, lb_im, d_skip, tri):
    n, w = u.shape
    t = SSM_CHUNK
    ns = pn_re.shape[1]
    const2 = lambda shape: pl.BlockSpec(shape, lambda i: (0, 0))
    const3 = lambda shape: pl.BlockSpec(shape, lambda i: (0, 0, 0))
    return pl.pallas_call(
        _ssm_kernel,
        grid=(n // t,),
        in_specs=[
            pl.BlockSpec((t, w), lambda i: (i, 0)),
            const3(bbd.shape), const3(cbd.shape),
            const2((t, ns)), const2((t, ns)), const2((t, ns)), const2((t, ns)),
            const2((1, ns)), const2((1, ns)), const2((1, w)), const2((t, t)),
        ],
        out_specs=pl.BlockSpec((t, w), lambda i: (i, 0)),
        out_shape=jax.ShapeDtypeStruct((n, w), F32),
        scratch_shapes=[pltpu.VMEM((1, ns), F32), pltpu.VMEM((1, ns), F32)],
        compiler_params=_params("arbitrary"),
        name="ssm_scan",
    )(u, bbd, cbd, pn_re, pn_im, pp_re, pp_im, lb_re, lb_im, d_skip, tri)


def _complex_powers(lr, li, count):
    pr, pi = jnp.ones_like(lr), jnp.zeros_like(li)
    cr, ci = lr, li
    rows = 1
    while rows < count:
        pr, pi = (jnp.concatenate([pr, pr * cr - pi * ci], axis=0),
                  jnp.concatenate([pi, pr * ci + pi * cr], axis=0))
        cr, ci = cr * cr - ci * ci, 2.0 * cr * ci
        rows *= 2
    return pr[:count], pi[:count]


def _ssm_tables(a_re, a_im, b_re, b_im, c_re, c_im, log_dt):
    g, p = a_re.shape
    h = b_re.shape[2]
    dt = jnp.exp(log_dt)[:, None]
    mag = jnp.exp(a_re * dt)
    lb_re = mag * jnp.cos(a_im * dt)
    lb_im = mag * jnp.sin(a_im * dt)
    nr, ni = lb_re - 1.0, lb_im
    den = a_re * a_re + a_im * a_im
    coef_re = (nr * a_re + ni * a_im) / den
    coef_im = (ni * a_re - nr * a_im) / den
    bb_re = coef_re[:, :, None] * b_re - coef_im[:, :, None] * b_im
    bb_im = coef_re[:, :, None] * b_im + coef_im[:, :, None] * b_re
    gs = g // SSM_SLABS
    eye = jnp.eye(gs, dtype=F32)

    def in_blocks(bb):
        t = bb.reshape(SSM_SLABS, gs, p, h).transpose(0, 1, 3, 2)
        return jnp.einsum("ab,sbhp->sahbp", eye, t).reshape(SSM_SLABS, gs * h, gs * p)

    def out_blocks(cc):
        t = cc.reshape(SSM_SLABS, gs, h, p).transpose(0, 1, 3, 2)
        return jnp.einsum("ab,sbph->sbpah", eye, t).reshape(SSM_SLABS, gs * p, gs * h)

    bbd = jnp.concatenate([in_blocks(bb_re), in_blocks(bb_im)], axis=2).astype(BF16)
    cbd = jnp.concatenate([out_blocks(c_re), -out_blocks(c_im)], axis=1).astype(BF16)
    lr, li = lb_re.reshape(1, g * p), lb_im.reshape(1, g * p)
    inv_den = lr * lr + li * li
    pp_re, pp_im = _complex_powers(lr, li, SSM_CHUNK)
    pn_re, pn_im = _complex_powers(lr / inv_den, -li / inv_den, SSM_CHUNK)
    return bbd, cbd, pn_re, pn_im, pp_re, pp_im, lr, li


def _attn_kernel(lam_ref, q_ref, k_ref, v_ref, g_ref, o_ref,
                 m1_ref, l1_ref, a1_ref, m2_ref, l2_ref, a2_ref, *, tk, out_scale):
    i = pl.program_id(1)
    tq = q_ref.shape[0]
    q = q_ref[...]
    maps = ((q[:, :HEAD_DIM], 0, m1_ref, l1_ref, a1_ref),
            (q[:, HEAD_DIM:], HEAD_DIM, m2_ref, l2_ref, a2_ref))
    for _, _, m_ref, l_ref, a_ref in maps:
        m_ref[...] = jnp.full_like(m_ref, NEG_INF)
        l_ref[...] = jnp.zeros_like(l_ref)
        a_ref[...] = jnp.zeros_like(a_ref)

    def step(j, masked):
        start = pl.multiple_of(j * tk, tk)
        kb = k_ref[pl.ds(start, tk), :]
        vb = v_ref[pl.ds(start, tk), :]
        for qq, off, m_ref, l_ref, a_ref in maps:
            s = lax.dot_general(qq, kb[:, off:off + HEAD_DIM], (((1,), (1,)), ((), ())),
                                preferred_element_type=F32)
            if masked:
                row = lax.broadcasted_iota(jnp.int32, s.shape, 0)
                col = lax.broadcasted_iota(jnp.int32, s.shape, 1)
                s = jnp.where(col <= row, s, NEG_INF)
            m_prev = m_ref[...]
            m_new = jnp.maximum(m_prev, jnp.max(s, axis=1, keepdims=True))
            alpha = jnp.exp(m_prev - m_new)
            p = jnp.exp(s - m_new)
            l_ref[...] = alpha * l_ref[...] + jnp.sum(p, axis=1, keepdims=True)
            a_ref[...] = alpha * a_ref[...] + jnp.dot(p.astype(BF16), vb, preferred_element_type=F32)
            m_ref[...] = m_new

    def body(j, carry):
        step(j, False)
        return carry

    assert tq == tk
    lax.fori_loop(0, i, body, 0)
    step(i, True)

    o = a1_ref[...] / l1_ref[...] - lam_ref[0, 0] * (a2_ref[...] / l2_ref[...])
    ms = jnp.mean(o * o, axis=-1, keepdims=True)
    o_ref[...] = (o * lax.rsqrt(ms + RMS_EPS) * g_ref[...] * out_scale).astype(o_ref.dtype)


def _attention(qk, v, lam, subln_g, lam_init):
    n = qk.shape[0]
    tq = min(ATTN_Q, n)
    tk = min(ATTN_K, n)
    hw = 2 * HEAD_DIM
    kernel = functools.partial(_attn_kernel, tk=tk, out_scale=1.0 - lam_init)
    return pl.pallas_call(
        kernel,
        grid=(N_HEADS, n // tq),
        in_specs=[
            pl.BlockSpec(memory_space=pltpu.SMEM),
            pl.BlockSpec((tq, hw), lambda h, i: (i, h)),
            pl.BlockSpec((n, hw), lambda h, i: (0, N_HEADS + h)),
            pl.BlockSpec((n, hw), lambda h, i: (0, h)),
            pl.BlockSpec((1, hw), lambda h, i: (0, 0)),
        ],
        out_specs=pl.BlockSpec((tq, hw), lambda h, i: (i, h)),
        out_shape=jax.ShapeDtypeStruct((n, V_WIDTH), BF16),
        scratch_shapes=[
            pltpu.VMEM((tq, 1), F32), pltpu.VMEM((tq, 1), F32), pltpu.VMEM((tq, hw), F32),
            pltpu.VMEM((tq, 1), F32), pltpu.VMEM((tq, 1), F32), pltpu.VMEM((tq, hw), F32),
        ],
        compiler_params=_params("parallel", "arbitrary"),
        name="diff_attn",
    )(lam, qk, qk, v, subln_g)


def _mix_kernel(x_ref, ys_ref, at_ref, sg_ref, wglu_ref, wbs_ref, wba_ref, wout_ref,
                g_ref, b_ref, o_ref):
    d = x_ref.shape[1]
    y = ys_ref[...]
    z = 0.5 * y * (1.0 + lax.erf(y * (2.0 ** -0.5)))
    gate = _sigmoid(jnp.dot(z.astype(BF16), wglu_ref[...], preferred_element_type=F32))
    glu = (z * gate).astype(BF16)
    ya = jnp.dot(glu, wbs_ref[...], preferred_element_type=F32)
    yb = jnp.dot(at_ref[...], wba_ref[...], preferred_element_type=F32)
    sg = sg_ref[...].astype(F32)
    mix = (sg[:, :d] * ya + sg[:, d:] * yb).astype(BF16)
    out = ALPHA * x_ref[...] + jnp.dot(mix, wout_ref[...], preferred_element_type=F32)
    o_ref[...] = _layer_norm(out, g_ref[...], b_ref[...])


def _resident(shape):
    return pl.BlockSpec(shape, lambda i: (0,) * len(shape), pipeline_mode=pl.Buffered(1))


def _mix(x, ys, attn, sg, wglu, wbs, wba, wout, g, b):
    n, d = x.shape
    tm = min(MIX_ROWS, n)
    rows = lambda w: pl.BlockSpec((tm, w), lambda i: (i, 0))
    return pl.pallas_call(
        _mix_kernel,
        grid=(n // tm,),
        in_specs=[
            rows(d), rows(ys.shape[1]), rows(attn.shape[1]), rows(sg.shape[1]),
            _resident(wglu.shape), _resident(wbs.shape), _resident(wba.shape), _resident(wout.shape),
            _resident((1, d)), _resident((1, d)),
        ],
        out_specs=rows(d),
        out_shape=jax.ShapeDtypeStruct((n, d), F32),
        compiler_params=_params("parallel"),
        name="mixer_out",
    )(x, ys, attn, sg, wglu, wbs, wba, wout, g, b)


def _ple_kernel(x_ref, p_ref, wg_ref, wp_ref, g_ref, b_ref, o_ref):
    x = x_ref[...]
    gate = _sigmoid(jnp.dot(x.astype(BF16), wg_ref[...], preferred_element_type=F32))
    emb = jnp.dot(p_ref[...].astype(BF16), wp_ref[...], preferred_element_type=F32)
    o_ref[...] = _layer_norm(ALPHA * x + gate * emb, g_ref[...], b_ref[...])


def _ple(x, p, wg, wp, g, b):
    n, d = x.shape
    tm = min(PLE_ROWS, n)
    rows = lambda w: pl.BlockSpec((tm, w), lambda i: (i, 0))
    return pl.pallas_call(
        _ple_kernel,
        grid=(n // tm,),
        in_specs=[rows(d), rows(p.shape[1]), _resident(wg.shape), _resident(wp.shape),
                  _resident((1, d)), _resident((1, d))],
        out_specs=rows(d),
        out_shape=jax.ShapeDtypeStruct((n, d), F32),
        compiler_params=_params("parallel"),
        name="ple_ln",
    )(x, p, wg, wp, g, b)


def _rope_tables(positions):
    half = ROT_DIM // 2
    inv_freq = ROPE_THETA ** (-jnp.arange(0, ROT_DIM, 2, dtype=F32) / ROT_DIM)
    ang = positions.astype(F32)[:, None] * inv_freq
    cos, sin = jnp.cos(ang), jnp.sin(ang)
    n = positions.shape[0]
    pad = jnp.zeros((n, HEAD_DIM - ROT_DIM), F32)
    zeros = jnp.zeros((n, half), F32)
    cos_t = jnp.concatenate([cos, cos, pad + 1.0], axis=1)
    sin_lo = jnp.concatenate([-sin, zeros, pad], axis=1)
    sin_hi = jnp.concatenate([zeros, sin, pad], axis=1)
    reps = LANES // HEAD_DIM
    return tuple(jnp.tile(t, (1, reps)) for t in (cos_t, sin_lo, sin_hi))


def kernel(x, p, positions, ffn1_w_gate, ffn1_w_up, ffn1_w_down, ln1_g, ln1_b, w_in, ssm_a_re, ssm_a_im, ssm_b_re, ssm_b_im, ssm_c_re, ssm_c_im, ssm_log_dt, ssm_d, ssm_w_glu, w_branch_ssm, lambda_q1, lambda_k1, lambda_q2, lambda_k2, attn_subln_g, w_branch_attn, w_out, ln2_g, ln2_b, ffn2_w_gate, ffn2_w_up, ffn2_w_down, ln3_g, ln3_b, ple_w_gate, ple_w_proj, ln4_g, ln4_b):
    bsz, length, d = x.shape
    assert bsz == 1
    xs = x.reshape(length, d)
    cos, slo, shi = _rope_tables(positions[0])
    tri = jnp.tril(jnp.ones((SSM_CHUNK, SSM_CHUNK), F32)).astype(BF16)
    row = lambda a: a.reshape(1, -1)
    for i in range(DEPTH):
        lam_init = 0.8 - 0.6 * math.exp(-0.3 * i)
        lam = (jnp.exp(jnp.sum(lambda_q1[i] * lambda_k1[i])) - jnp.exp(jnp.sum(lambda_q2[i] * lambda_k2[i]))
               + lam_init).reshape(1, 1)
        xs = _ffn_ln(xs, ffn1_w_gate[i].astype(BF16), ffn1_w_up[i].astype(BF16),
                     ffn1_w_down[i].astype(BF16), row(ln1_g[i]), row(ln1_b[i]))
        u, qk, v, sg = _inproj(xs, w_in[i].astype(BF16), cos, slo, shi)
        tables = _ssm_tables(ssm_a_re[i], ssm_a_im[i], ssm_b_re[i], ssm_b_im[i],
                             ssm_c_re[i], ssm_c_im[i], ssm_log_dt[i])
        ys = _ssm(u, *tables, row(ssm_d[i]), tri)
        attn = _attention(qk, v, lam, row(attn_subln_g[i]), lam_init)
        xs = _mix(xs, ys, attn, sg, ssm_w_glu[i].astype(BF16), w_branch_ssm[i].astype(BF16),
                  w_branch_attn[i].astype(BF16), w_out[i].astype(BF16), row(ln2_g[i]), row(ln2_b[i]))
        xs = _ffn_ln(xs, ffn2_w_gate[i].astype(BF16), ffn2_w_up[i].astype(BF16),
                     ffn2_w_down[i].astype(BF16), row(ln3_g[i]), row(ln3_b[i]))
        xs = _ple(xs, p[i, 0], ple_w_gate[i].astype(BF16), ple_w_proj[i].astype(BF16),
                  row(ln4_g[i]), row(ln4_b[i]))
    return xs.reshape(bsz, length, d)
```

```python
import functools
import math

import jax
import jax.numpy as jnp
from jax import lax
from jax.experimental import pallas as pl
from jax.experimental.pallas import tpu as pltpu

F32 = jnp.float32
BF16 = jnp.bfloat16

DEPTH = 2
SSM_WIDTH = 1024
SSM_GROUP = 16
SSM_GROUPS = SSM_WIDTH // SSM_GROUP
SSM_STATE = 64
N_HEADS = 8
HEAD_DIM = 64
QK_WIDTH = N_HEADS * 2 * HEAD_DIM
V_WIDTH = N_HEADS * 2 * HEAD_DIM
ROT_DIM = HEAD_DIM // 4
ROPE_THETA = 500000.0
LN_EPS = 1e-5
RMS_EPS = 1e-5
NEG_INF = -1e30
ALPHA = (2 * DEPTH) ** 0.25
Q_SCALE = HEAD_DIM ** -0.5 * math.log2(math.e)

LANES = 128
VMEM_LIMIT_BYTES = 56 * 1024 * 1024

FFN_ROWS = 512
FFN_COLS = 512
PROJ_ROWS = 512
PROJ_COLS = 1024
SSM_CHUNK = 128
SSM_SLABS = 4
ATTN_Q = 512
ATTN_K = 512
MIX_ROWS = 256
PLE_ROWS = 512


def _sigmoid(x):
    return 1.0 / (1.0 + jnp.exp(-x))


def _layer_norm(y, g, b):
    mu = jnp.mean(y, axis=-1, keepdims=True)
    yc = y - mu
    var = jnp.mean(yc * yc, axis=-1, keepdims=True)
    return yc * lax.rsqrt(var + LN_EPS) * g + b


def _params(*semantics):
    return pltpu.CompilerParams(dimension_semantics=semantics, vmem_limit_bytes=VMEM_LIMIT_BYTES)


def _ffn_ln_kernel(x_ref, wg_ref, wu_ref, wd_ref, g_ref, b_ref, o_ref, xb_ref, acc_ref):
    j = pl.program_id(1)

    @pl.when(j == 0)
    def _():
        xb_ref[...] = x_ref[...].astype(BF16)

    xb = xb_ref[...]
    hg = jnp.dot(xb, wg_ref[...], preferred_element_type=F32)
    hu = jnp.dot(xb, wu_ref[...], preferred_element_type=F32)
    act = (hg * _sigmoid(hg) * hu).astype(BF16)
    part = jnp.dot(act, wd_ref[...], preferred_element_type=F32)

    @pl.when(j == 0)
    def _():
        acc_ref[...] = part

    @pl.when(j > 0)
    def _():
        acc_ref[...] += part

    @pl.when(j == pl.num_programs(1) - 1)
    def _():
        y = ALPHA * x_ref[...] + 0.5 * acc_ref[...]
        o_ref[...] = _layer_norm(y, g_ref[...], b_ref[...])


def _ffn_ln(x, wg, wu, wd, g, b):
    n, d = x.shape
    f = wg.shape[1]
    tm = min(FFN_ROWS, n)
    tf = FFN_COLS
    return pl.pallas_call(
        _ffn_ln_kernel,
        grid=(n // tm, f // tf),
        in_specs=[
            pl.BlockSpec((tm, d), lambda i, j: (i, 0)),
            pl.BlockSpec((d, tf), lambda i, j: (0, j)),
            pl.BlockSpec((d, tf), lambda i, j: (0, j)),
            pl.BlockSpec((tf, d), lambda i, j: (j, 0)),
            pl.BlockSpec((1, d), lambda i, j: (0, 0)),
            pl.BlockSpec((1, d), lambda i, j: (0, 0)),
        ],
        out_specs=pl.BlockSpec((tm, d), lambda i, j: (i, 0)),
        out_shape=jax.ShapeDtypeStruct((n, d), F32),
        scratch_shapes=[pltpu.VMEM((tm, d), BF16), pltpu.VMEM((tm, d), F32)],
        compiler_params=_params("parallel", "arbitrary"),
        name="ffn_ln",
    )(x, wg, wu, wd, g, b)


def _rotary(h, cos, sin_lo, sin_hi):
    half = ROT_DIM // 2
    return (h * cos + pltpu.roll(h, half, 1) * sin_hi
            + pltpu.roll(h, LANES - half, 1) * sin_lo)


def _inproj_kernel(x_ref, w_ref, cos_ref, slo_ref, shi_ref, u_ref, qt_ref, k_ref, vt_ref, g_ref, xb_ref):
    j = pl.program_id(1)

    @pl.when(j == 0)
    def _():
        xb_ref[...] = x_ref[...].astype(BF16)

    h = jnp.dot(xb_ref[...], w_ref[...], preferred_element_type=F32)

    @pl.when(j == 0)
    def _():
        u_ref[...] = h

    def rope(c):
        return _rotary(h[:, c * LANES:(c + 1) * LANES], cos_ref[...], slo_ref[...], shi_ref[...])

    @pl.when(j == 1)
    def _():
        for c in range(h.shape[1] // LANES):
            qt_ref[c * LANES:(c + 1) * LANES, :] = (rope(c) * Q_SCALE).T.astype(BF16)

    @pl.when(j == 2)
    def _():
        for c in range(h.shape[1] // LANES):
            k_ref[:, c * LANES:(c + 1) * LANES] = rope(c).astype(BF16)

    @pl.when(j == 3)
    def _():
        for c in range(h.shape[1] // LANES):
            vt_ref[c * LANES:(c + 1) * LANES, :] = h[:, c * LANES:(c + 1) * LANES].T.astype(BF16)

    @pl.when(j >= 4)
    def _():
        g_ref[...] = _sigmoid(h).astype(BF16)


def _inproj(x, w, cos, slo, shi):
    n, d = x.shape
    tm = min(PROJ_ROWS, n)
    tn = PROJ_COLS
    width = w.shape[1]
    n_gate = (width - SSM_WIDTH - 2 * QK_WIDTH - V_WIDTH) // tn
    assert SSM_WIDTH == tn and QK_WIDTH == tn and V_WIDTH == tn
    tab = pl.BlockSpec((tm, LANES), lambda i, j: (i, 0))
    return pl.pallas_call(
        _inproj_kernel,
        grid=(n // tm, width // tn),
        in_specs=[
            pl.BlockSpec((tm, d), lambda i, j: (i, 0)),
            pl.BlockSpec((d, tn), lambda i, j: (0, j)),
            tab, tab, tab,
        ],
        out_specs=[
            pl.BlockSpec((tm, tn), lambda i, j: (i, 0)),
            pl.BlockSpec((tn, tm), lambda i, j: (0, i)),
            pl.BlockSpec((tm, tn), lambda i, j: (i, 0)),
            pl.BlockSpec((tn, tm), lambda i, j: (0, i)),
            pl.BlockSpec((tm, tn), lambda i, j: (i, jnp.clip(j - 4, 0, n_gate - 1))),
        ],
        out_shape=[
            jax.ShapeDtypeStruct((n, SSM_WIDTH), F32),
            jax.ShapeDtypeStruct((QK_WIDTH, n), BF16),
            jax.ShapeDtypeStruct((n, QK_WIDTH), BF16),
            jax.ShapeDtypeStruct((V_WIDTH, n), BF16),
            jax.ShapeDtypeStruct((n, n_gate * tn), BF16),
        ],
        scratch_shapes=[pltpu.VMEM((tm, d), BF16)],
        compiler_params=_params("parallel", "arbitrary"),
        name="inproj",
    )(x, w, cos, slo, shi)


def _ssm_kernel(u_ref, bbd_ref, cbd_ref, pn_re_ref, pn_im_ref, pp_re_ref, pp_im_ref,
                lb_re_ref, lb_im_ref, d_ref, tri_ref, y_ref, car_re_ref, car_im_ref):
    @pl.when(pl.program_id(0) == 0)
    def _():
        car_re_ref[...] = jnp.zeros_like(car_re_ref)
        car_im_ref[...] = jnp.zeros_like(car_im_ref)

    t_len = u_ref.shape[0]
    slab_in = u_ref.shape[1] // SSM_SLABS
    slab_st = pn_re_ref.shape[1] // SSM_SLABS
    u = u_ref[...]
    ub = u.astype(BF16)
    tri = tri_ref[...]
    for s in range(SSM_SLABS):
        cin = slice(s * slab_in, (s + 1) * slab_in)
        cst = slice(s * slab_st, (s + 1) * slab_st)
        bu = jnp.dot(ub[:, cin], bbd_ref[s], preferred_element_type=F32)
        br, bi = bu[:, :slab_st], bu[:, slab_st:]
        nr, ni = pn_re_ref[:, cst], pn_im_ref[:, cst]
        z = jnp.concatenate([br * nr - bi * ni, br * ni + bi * nr], axis=1)
        z_hi = z.astype(BF16)
        z_lo = (z - z_hi.astype(F32)).astype(BF16)
        cs = (jnp.dot(tri, z_hi, preferred_element_type=F32)
              + jnp.dot(tri, z_lo, preferred_element_type=F32))
        lr, li = lb_re_ref[:, cst], lb_im_ref[:, cst]
        kr, ki = car_re_ref[:, cst], car_im_ref[:, cst]
        cr = cs[:, :slab_st] + (lr * kr - li * ki)
        ci = cs[:, slab_st:] + (lr * ki + li * kr)
        pr, pi = pp_re_ref[:, cst], pp_im_ref[:, cst]
        sr = cr * pr - ci * pi
        si = cr * pi + ci * pr
        car_re_ref[:, cst] = sr[t_len - 1:t_len, :]
        car_im_ref[:, cst] = si[t_len - 1:t_len, :]
        sb = jnp.concatenate([sr, si], axis=1).astype(BF16)
        y = jnp.dot(sb, cbd_ref[s], preferred_element_type=F32)
        y_ref[:, cin] = y + d_ref[:, cin] * u[:, cin]


def _ssm(u, bbd, cbd, pn_re, pn_im, pp_re, pp_im, lb_re, lb_im, d_skip, tri):
    n, w = u.shape
    t = SSM_CHUNK
    ns = pn_re.shape[1]
    const2 = lambda shape: pl.BlockSpec(shape, lambda i: (0, 0))
    const3 = lambda shape: pl.BlockSpec(shape, lambda i: (0, 0, 0))
    return pl.pallas_call(
        _ssm_kernel,
        grid=(n // t,),
        in_specs=[
            pl.BlockSpec((t, w), lambda i: (i, 0)),
            const3(bbd.shape), const3(cbd.shape),
            const2((t, ns)), const2((t, ns)), const2((t, ns)), const2((t, ns)),
            const2((1, ns)), const2((1, ns)), const2((1, w)), const2((t, t)),
        ],
        out_specs=pl.BlockSpec((t, w), lambda i: (i, 0)),
        out_shape=jax.ShapeDtypeStruct((n, w), F32),
        scratch_shapes=[pltpu.VMEM((1, ns), F32), pltpu.VMEM((1, ns), F32)],
        compiler_params=_params("arbitrary"),
        name="ssm_scan",
    )(u, bbd, cbd, pn_re, pn_im, pp_re, pp_im, lb_re, lb_im, d_skip, tri)


def _complex_powers(lr, li, count):
    pr, pi = jnp.ones_like(lr), jnp.zeros_like(li)
    cr, ci = lr, li
    rows = 1
    while rows < count:
        pr, pi = (jnp.concatenate([pr, pr * cr - pi * ci], axis=0),
                  jnp.concatenate([pi, pr * ci + pi * cr], axis=0))
        cr, ci = cr * cr - ci * ci, 2.0 * cr * ci
        rows *= 2
    return pr[:count], pi[:count]


def _ssm_tables(a_re, a_im, b_re, b_im, c_re, c_im, log_dt):
    g, p = a_re.shape
    h = b_re.shape[2]
    dt = jnp.exp(log_dt)[:, None]
    mag = jnp.exp(a_re * dt)
    lb_re = mag * jnp.cos(a_im * dt)
    lb_im = mag * jnp.sin(a_im * dt)
    nr, ni = lb_re - 1.0, lb_im
    den = a_re * a_re + a_im * a_im
    coef_re = (nr * a_re + ni * a_im) / den
    coef_im = (ni * a_re - nr * a_im) / den
    bb_re = coef_re[:, :, None] * b_re - coef_im[:, :, None] * b_im
    bb_im = coef_re[:, :, None] * b_im + coef_im[:, :, None] * b_re
    gs = g // SSM_SLABS
    eye = jnp.eye(gs, dtype=F32)

    def in_blocks(bb):
        t = bb.reshape(SSM_SLABS, gs, p, h).transpose(0, 1, 3, 2)
        return jnp.einsum("ab,sbhp->sahbp", eye, t).reshape(SSM_SLABS, gs * h, gs * p)

    def out_blocks(cc):
        t = cc.reshape(SSM_SLABS, gs, h, p).transpose(0, 1, 3, 2)
        return jnp.einsum("ab,sbph->sbpah", eye, t).reshape(SSM_SLABS, gs * p, gs * h)

    bbd = jnp.concatenate([in_blocks(bb_re), in_blocks(bb_im)], axis=2).astype(BF16)
    cbd = jnp.concatenate([out_blocks(c_re), -out_blocks(c_im)], axis=1).astype(BF16)
    lr, li = lb_re.reshape(1, g * p), lb_im.reshape(1, g * p)
    inv_den = lr * lr + li * li
    pp_re, pp_im = _complex_powers(lr, li, SSM_CHUNK)
    pn_re, pn_im = _complex_powers(lr / inv_den, -li / inv_den, SSM_CHUNK)
    return bbd, cbd, pn_re, pn_im, pp_re, pp_im, lr, li


def _attn_kernel(lam_ref, qt_ref, k_ref, vt_ref, g_ref, o_ref,
                 sa_ref, sb_ref, mxa_ref, mxb_ref, m_ref, l_ref, acc_ref, *, tk, out_scale):
    i = pl.program_id(1)
    tq = qt_ref.shape[1]
    assert tq == tk
    qt = qt_ref[...]
    comp = lax.broadcasted_iota(jnp.int32, qt.shape, 0) < HEAD_DIM
    zero = jnp.zeros_like(qt)
    qz = (jnp.where(comp, qt, zero), jnp.where(comp, zero, qt))
    m_ref[...] = jnp.full_like(m_ref, NEG_INF)
    l_ref[...] = jnp.zeros_like(l_ref)
    acc_ref[...] = jnp.zeros_like(acc_ref)

    def scores(j, s_ref, mx_ref, masked):
        start = pl.multiple_of(j * tk, tk)
        kb = k_ref[pl.ds(start, tk), :]
        for c in range(2):
            s = jnp.dot(kb, qz[c], preferred_element_type=F32)
            if masked:
                key = lax.broadcasted_iota(jnp.int32, s.shape, 0)
                qry = lax.broadcasted_iota(jnp.int32, s.shape, 1)
                s = jnp.where(key <= qry, s, NEG_INF)
            s_ref[c] = s
            mx_ref[c] = jnp.max(s, axis=0, keepdims=True)

    def update(j, s_ref, mx_ref):
        start = pl.multiple_of(j * tk, tk)
        vtb = vt_ref[:, pl.ds(start, tk)]
        for c in range(2):
            m_prev = m_ref[c]
            m_new = jnp.maximum(m_prev, mx_ref[c])
            alpha = jnp.exp2(m_prev - m_new)
            p = jnp.exp2(s_ref[c] - m_new)
            l_ref[c] = alpha * l_ref[c] + jnp.sum(p, axis=0, keepdims=True)
            acc_ref[c] = alpha * acc_ref[c] + jnp.dot(vtb, p.astype(BF16), preferred_element_type=F32)
            m_ref[c] = m_new

    @pl.when(i == 0)
    def _():
        scores(0, sa_ref, mxa_ref, True)
        update(0, sa_ref, mxa_ref)

    @pl.when(i > 0)
    def _():
        scores(0, sa_ref, mxa_ref, False)

    def pair(jj, carry):
        a = 2 * jj
        scores(a + 1, sb_ref, mxb_ref, False)
        update(a, sa_ref, mxa_ref)
        scores(a + 2, sa_ref, mxa_ref, False)
        update(a + 1, sb_ref, mxb_ref)
        return carry

    n_pairs = lax.shift_right_logical(jnp.maximum(i - 1, 0), 1)
    lax.fori_loop(0, n_pairs, pair, 0)
    first = 2 * n_pairs

    @pl.when(i % 2 == 1)
    def _():
        scores(i, sb_ref, mxb_ref, True)
        update(first, sa_ref, mxa_ref)
        update(i, sb_ref, mxb_ref)

    @pl.when((i % 2 == 0) & (i > 0))
    def _():
        scores(first + 1, sb_ref, mxb_ref, False)
        update(first, sa_ref, mxa_ref)
        scores(i, sa_ref, mxa_ref, True)
        update(first + 1, sb_ref, mxb_ref)
        update(i, sa_ref, mxa_ref)

    ot = acc_ref[0] / l_ref[0] - lam_ref[0, 0] * (acc_ref[1] / l_ref[1])
    ms = jnp.mean(ot * ot, axis=0, keepdims=True)
    o = (ot * lax.rsqrt(ms + RMS_EPS)).T
    o_ref[...] = (o * g_ref[...] * out_scale).astype(o_ref.dtype)


def _attention(qt, k, vt, lam, subln_g, lam_init):
    n = k.shape[0]
    tq = min(ATTN_Q, n)
    tk = min(ATTN_K, n)
    hw = 2 * HEAD_DIM
    kernel = functools.partial(_attn_kernel, tk=tk, out_scale=1.0 - lam_init)
    return pl.pallas_call(
        kernel,
        grid=(N_HEADS, n // tq),
        in_specs=[
            pl.BlockSpec(memory_space=pltpu.SMEM),
            pl.BlockSpec((hw, tq), lambda h, i: (h, i)),
            pl.BlockSpec((n, hw), lambda h, i: (0, h)),
            pl.BlockSpec((hw, n), lambda h, i: (h, 0)),
            pl.BlockSpec((1, hw), lambda h, i: (0, 0)),
        ],
        out_specs=pl.BlockSpec((tq, hw), lambda h, i: (i, h)),
        out_shape=jax.ShapeDtypeStruct((n, V_WIDTH), BF16),
        scratch_shapes=[
            pltpu.VMEM((2, tk, tq), F32), pltpu.VMEM((2, tk, tq), F32),
            pltpu.VMEM((2, 1, tq), F32), pltpu.VMEM((2, 1, tq), F32),
            pltpu.VMEM((2, 1, tq), F32), pltpu.VMEM((2, 1, tq), F32),
            pltpu.VMEM((2, hw, tq), F32),
        ],
        compiler_params=_params("parallel", "arbitrary"),
        name="diff_attn",
    )(lam, qt, k, vt, subln_g)


def _mix_kernel(x_ref, ys_ref, at_ref, sg_ref, wglu_ref, wbs_ref, wba_ref, wout_ref,
                g_ref, b_ref, o_ref):
    d = x_ref.shape[1]
    y = ys_ref[...]
    z = 0.5 * y * (1.0 + lax.erf(y * (2.0 ** -0.5)))
    gate = _sigmoid(jnp.dot(z.astype(BF16), wglu_ref[...], preferred_element_type=F32))
    glu = (z * gate).astype(BF16)
    ya = jnp.dot(glu, wbs_ref[...], preferred_element_type=F32)
    yb = jnp.dot(at_ref[...], wba_ref[...], preferred_element_type=F32)
    sg = sg_ref[...].astype(F32)
    mix = (sg[:, :d] * ya + sg[:, d:] * yb).astype(BF16)
    out = ALPHA * x_ref[...] + jnp.dot(mix, wout_ref[...], preferred_element_type=F32)
    o_ref[...] = _layer_norm(out, g_ref[...], b_ref[...])


def _resident(shape):
    return pl.BlockSpec(shape, lambda i: (0,) * len(shape), pipeline_mode=pl.Buffered(1))


def _mix(x, ys, attn, sg, wglu, wbs, wba, wout, g, b):
    n, d = x.shape
    tm = min(MIX_ROWS, n)
    rows = lambda w: pl.BlockSpec((tm, w), lambda i: (i, 0))
    return pl.pallas_call(
        _mix_kernel,
        grid=(n // tm,),
        in_specs=[
            rows(d), rows(ys.shape[1]), rows(attn.shape[1]), rows(sg.shape[1]),
            _resident(wglu.shape), _resident(wbs.shape), _resident(wba.shape), _resident(wout.shape),
            _resident((1, d)), _resident((1, d)),
        ],
        out_specs=rows(d),
        out_shape=jax.ShapeDtypeStruct((n, d), F32),
        compiler_params=_params("parallel"),
        name="mixer_out",
    )(x, ys, attn, sg, wglu, wbs, wba, wout, g, b)


def _ple_kernel(x_ref, p_ref, wg_ref, wp_ref, g_ref, b_ref, o_ref):
    x = x_ref[...]
    gate = _sigmoid(jnp.dot(x.astype(BF16), wg_ref[...], preferred_element_type=F32))
    emb = jnp.dot(p_ref[...].astype(BF16), wp_ref[...], preferred_element_type=F32)
    o_ref[...] = _layer_norm(ALPHA * x + gate * emb, g_ref[...], b_ref[...])


def _ple(x, p, wg, wp, g, b):
    n, d = x.shape
    tm = min(PLE_ROWS, n)
    rows = lambda w: pl.BlockSpec((tm, w), lambda i: (i, 0))
    return pl.pallas_call(
        _ple_kernel,
        grid=(n // tm,),
        in_specs=[rows(d), rows(p.shape[1]), _resident(wg.shape), _resident(wp.shape),
                  _resident((1, d)), _resident((1, d))],
        out_specs=rows(d),
        out_shape=jax.ShapeDtypeStruct((n, d), F32),
        compiler_params=_params("parallel"),
        name="ple_ln",
    )(x, p, wg, wp, g, b)


def _rope_tables(positions):
    half = ROT_DIM // 2
    inv_freq = ROPE_THETA ** (-jnp.arange(0, ROT_DIM, 2, dtype=F32) / ROT_DIM)
    ang = positions.astype(F32)[:, None] * inv_freq
    cos, sin = jnp.cos(ang), jnp.sin(ang)
    n = positions.shape[0]
    pad = jnp.zeros((n, HEAD_DIM - ROT_DIM), F32)
    zeros = jnp.zeros((n, half), F32)
    cos_t = jnp.concatenate([cos, cos, pad + 1.0], axis=1)
    sin_lo = jnp.concatenate([-sin, zeros, pad], axis=1)
    sin_hi = jnp.concatenate([zeros, sin, pad], axis=1)
    reps = LANES // HEAD_DIM
    return tuple(jnp.tile(t, (1, reps)) for t in (cos_t, sin_lo, sin_hi))


def kernel(x, p, positions, ffn1_w_gate, ffn1_w_up, ffn1_w_down, ln1_g, ln1_b, w_in, ssm_a_re, ssm_a_im, ssm_b_re, ssm_b_im, ssm_c_re, ssm_c_im, ssm_log_dt, ssm_d, ssm_w_glu, w_branch_ssm, lambda_q1, lambda_k1, lambda_q2, lambda_k2, attn_subln_g, w_branch_attn, w_out, ln2_g, ln2_b, ffn2_w_gate, ffn2_w_up, ffn2_w_down, ln3_g, ln3_b, ple_w_gate, ple_w_proj, ln4_g, ln4_b):
    bsz, length, d = x.shape
    assert bsz == 1
    xs = x.reshape(length, d)
    cos, slo, shi = _rope_tables(positions[0])
    tri = jnp.tril(jnp.ones((SSM_CHUNK, SSM_CHUNK), F32)).astype(BF16)
    row = lambda a: a.reshape(1, -1)
    for i in range(DEPTH):
        lam_init = 0.8 - 0.6 * math.exp(-0.3 * i)
        lam = (jnp.exp(jnp.sum(lambda_q1[i] * lambda_k1[i])) - jnp.exp(jnp.sum(lambda_q2[i] * lambda_k2[i]))
               + lam_init).reshape(1, 1)
        xs = _ffn_ln(xs, ffn1_w_gate[i].astype(BF16), ffn1_w_up[i].astype(BF16),
                     ffn1_w_down[i].astype(BF16), row(ln1_g[i]), row(ln1_b[i]))
        u, qt, k, vt, sg = _inproj(xs, w_in[i].astype(BF16), cos, slo, shi)
        tables = _ssm_tables(ssm_a_re[i], ssm_a_im[i], ssm_b_re[i], ssm_b_im[i],
                             ssm_c_re[i], ssm_c_im[i], ssm_log_dt[i])
        ys = _ssm(u, *tables, row(ssm_d[i]), tri)
        attn = _attention(qt, k, vt, lam, row(attn_subln_g[i]), lam_init)
        xs = _mix(xs, ys, attn, sg, ssm_w_glu[i].astype(BF16), w_branch_ssm[i].astype(BF16),
                  w_branch_attn[i].astype(BF16), w_out[i].astype(BF16), row(ln2_g[i]), row(ln2_b[i]))
        xs = _ffn_ln(xs, ffn2_w_gate[i].astype(BF16), ffn2_w_up[i].astype(BF16),
                     ffn2_w_down[i].astype(BF16), row(ln3_g[i]), row(ln3_b[i]))
        xs = _ple(xs, p[i, 0], ple_w_gate[i].astype(BF16), ple_w_proj[i].astype(BF16),
                  row(ln4_g[i]), row(ln4_b[i]))
    return xs.reshape(bsz, length, d)
```

```python
import functools
import math

import jax
import jax.numpy as jnp
from jax import lax
from jax.experimental import pallas as pl
from jax.experimental.pallas import tpu as pltpu

F32 = jnp.float32
BF16 = jnp.bfloat16

DEPTH = 2
SSM_WIDTH = 1024
SSM_GROUP = 16
SSM_GROUPS = SSM_WIDTH // SSM_GROUP
SSM_STATE = 64
N_HEADS = 8
HEAD_DIM = 64
QK_WIDTH = N_HEADS * 2 * HEAD_DIM
V_WIDTH = N_HEADS * 2 * HEAD_DIM
ROT_DIM = HEAD_DIM // 4
ROPE_THETA = 500000.0
LN_EPS = 1e-5
RMS_EPS = 1e-5
NEG_INF = -1e30
ALPHA = (2 * DEPTH) ** 0.25
Q_SCALE = HEAD_DIM ** -0.5 * math.log2(math.e)

LANES = 128
BF16_ROWS = 16
VT_ROWS = 2 * HEAD_DIM + BF16_ROWS
VMEM_LIMIT_BYTES = 56 * 1024 * 1024

FFN_ROWS = 512
FFN_COLS = 512
PROJ_ROWS = 512
PROJ_COLS = 1024
PROJ_CHUNK = 256
SSM_CHUNK = 128
SSM_SLABS = 4
ATTN_Q = 512
ATTN_K = 512
MIX_ROWS = 256
PLE_ROWS = 512


def _sigmoid(x):
    return 1.0 / (1.0 + jnp.exp(-x))


def _layer_norm(y, g, b):
    mu = jnp.mean(y, axis=-1, keepdims=True)
    yc = y - mu
    var = jnp.mean(yc * yc, axis=-1, keepdims=True)
    return yc * lax.rsqrt(var + LN_EPS) * g + b


def _params(*semantics):
    return pltpu.CompilerParams(dimension_semantics=semantics, vmem_limit_bytes=VMEM_LIMIT_BYTES)


def _ffn_ln_kernel(x_ref, wg_ref, wu_ref, wd_ref, g_ref, b_ref, o_ref, xb_ref, acc_ref):
    j = pl.program_id(1)

    @pl.when(j == 0)
    def _():
        xb_ref[...] = x_ref[...].astype(BF16)
        acc_ref[...] = jnp.zeros_like(acc_ref)

    xb = xb_ref[...]
    hg = jnp.dot(xb, wg_ref[...], preferred_element_type=F32)
    hu = jnp.dot(xb, wu_ref[...], preferred_element_type=F32)
    act = (hg * _sigmoid(hg) * hu).astype(BF16)
    acc_ref[...] += jnp.dot(act, wd_ref[...], preferred_element_type=F32)

    @pl.when(j == pl.num_programs(1) - 1)
    def _():
        y = ALPHA * x_ref[...] + acc_ref[...]
        o_ref[...] = _layer_norm(y, g_ref[...], b_ref[...])


def _ffn_ln(x, wg, wu, wd, g, b):
    n, d = x.shape
    f = wg.shape[1]
    tm = min(FFN_ROWS, n)
    tf = FFN_COLS
    return pl.pallas_call(
        _ffn_ln_kernel,
        grid=(n // tm, f // tf),
        in_specs=[
            pl.BlockSpec((tm, d), lambda i, j: (i, 0)),
            pl.BlockSpec((d, tf), lambda i, j: (0, j)),
            pl.BlockSpec((d, tf), lambda i, j: (0, j)),
            pl.BlockSpec((tf, d), lambda i, j: (j, 0)),
            pl.BlockSpec((1, d), lambda i, j: (0, 0)),
            pl.BlockSpec((1, d), lambda i, j: (0, 0)),
        ],
        out_specs=pl.BlockSpec((tm, d), lambda i, j: (i, 0)),
        out_shape=jax.ShapeDtypeStruct((n, d), F32),
        scratch_shapes=[pltpu.VMEM((tm, d), BF16), pltpu.VMEM((tm, d), F32)],
        compiler_params=_params("parallel", "arbitrary"),
        name="ffn_ln",
    )(x, wg, wu, wd, g, b)


def _rotary(h, cos, sin_lo, sin_hi):
    half = ROT_DIM // 2
    return (h * cos + pltpu.roll(h, half, 1) * sin_hi
            + pltpu.roll(h, LANES - half, 1) * sin_lo)


def _inproj_kernel(x_ref, w_ref, cos_ref, slo_ref, shi_ref, u_ref, qt_ref, k_ref, vt_ref, g_ref, xb_ref):
    j = pl.program_id(1)

    @pl.when(j == 0)
    def _():
        xb_ref[...] = x_ref[...].astype(BF16)

    def for_chunks(store):
        xb = xb_ref[...]
        for c in range(w_ref.shape[1] // PROJ_CHUNK):
            cols = slice(c * PROJ_CHUNK, (c + 1) * PROJ_CHUNK)
            h = jnp.dot(xb, w_ref[:, cols], preferred_element_type=F32)
            for s in range(PROJ_CHUNK // LANES):
                lo = c * PROJ_CHUNK + s * LANES
                store(slice(lo, lo + LANES), h[:, s * LANES:(s + 1) * LANES])

    def rope(h):
        return _rotary(h, cos_ref[...], slo_ref[...], shi_ref[...])

    @pl.when(j == 0)
    def _():
        def store(cols, h):
            u_ref[:, cols] = h
        for_chunks(store)

    @pl.when(j == 1)
    def _():
        def store(cols, h):
            qt_ref[cols, :] = (rope(h) * Q_SCALE).T.astype(BF16)
        for_chunks(store)

    @pl.when(j == 2)
    def _():
        def store(cols, h):
            k_ref[:, cols] = rope(h).astype(BF16)
        for_chunks(store)

    @pl.when(j == 3)
    def _():
        def store(cols, h):
            base = cols.start // LANES * VT_ROWS
            vt_ref[base:base + LANES, :] = h.T.astype(BF16)
            vt_ref[base + LANES:base + VT_ROWS, :] = jnp.ones((VT_ROWS - LANES, h.shape[0]), BF16)
        for_chunks(store)

    @pl.when(j >= 4)
    def _():
        def store(cols, h):
            g_ref[:, cols] = _sigmoid(h).astype(BF16)
        for_chunks(store)


def _inproj(x, w, cos, slo, shi):
    n, d = x.shape
    tm = min(PROJ_ROWS, n)
    tn = PROJ_COLS
    width = w.shape[1]
    n_gate = (width - SSM_WIDTH - 2 * QK_WIDTH - V_WIDTH) // tn
    assert SSM_WIDTH == tn and QK_WIDTH == tn and V_WIDTH == tn
    tab = pl.BlockSpec((tm, LANES), lambda i, j: (i, 0))
    return pl.pallas_call(
        _inproj_kernel,
        grid=(n // tm, width // tn),
        in_specs=[
            pl.BlockSpec((tm, d), lambda i, j: (i, 0)),
            pl.BlockSpec((d, tn), lambda i, j: (0, j)),
            tab, tab, tab,
        ],
        out_specs=[
            pl.BlockSpec((tm, tn), lambda i, j: (i, 0)),
            pl.BlockSpec((tn, tm), lambda i, j: (0, i)),
            pl.BlockSpec((tm, tn), lambda i, j: (i, 0)),
            pl.BlockSpec((N_HEADS * VT_ROWS, tm), lambda i, j: (0, i)),
            pl.BlockSpec((tm, tn), lambda i, j: (i, jnp.clip(j - 4, 0, n_gate - 1))),
        ],
        out_shape=[
            jax.ShapeDtypeStruct((n, SSM_WIDTH), F32),
            jax.ShapeDtypeStruct((QK_WIDTH, n), BF16),
            jax.ShapeDtypeStruct((n, QK_WIDTH), BF16),
            jax.ShapeDtypeStruct((N_HEADS * VT_ROWS, n), BF16),
            jax.ShapeDtypeStruct((n, n_gate * tn), BF16),
        ],
        scratch_shapes=[pltpu.VMEM((tm, d), BF16)],
        compiler_params=_params("parallel", "arbitrary"),
        name="inproj",
    )(x, w, cos, slo, shi)


def _ssm_kernel(u_ref, bbd_ref, cbd_ref, pn_re_ref, pn_im_ref, pp_re_ref, pp_im_ref,
                lb_re_ref, lb_im_ref, d_ref, tri_ref, y_ref, car_re_ref, car_im_ref):
    @pl.when(pl.program_id(0) == 0)
    def _():
        car_re_ref[...] = jnp.zeros_like(car_re_ref)
        car_im_ref[...] = jnp.zeros_like(car_im_ref)

    t_len = u_ref.shape[0]
    slab_in = u_ref.shape[1] // SSM_SLABS
    slab_st = pn_re_ref.shape[1] // SSM_SLABS
    u = u_ref[...]
    ub = u.astype(BF16)
    tri = tri_ref[...]
    for s in range(SSM_SLABS):
        cin = slice(s * slab_in, (s + 1) * slab_in)
        cst = slice(s * slab_st, (s + 1) * slab_st)
        bu = jnp.dot(ub[:, cin], bbd_ref[s], preferred_element_type=F32)
        br, bi = bu[:, :slab_st], bu[:, slab_st:]
        nr, ni = pn_re_ref[:, cst], pn_im_ref[:, cst]
        z = jnp.concatenate([br * nr - bi * ni, br * ni + bi * nr], axis=1)
        z_hi = z.astype(BF16)
        z_lo = (z - z_hi.astype(F32)).astype(BF16)
        cs = (jnp.dot(tri, z_hi, preferred_element_type=F32)
              + jnp.dot(tri, z_lo, preferred_element_type=F32))
        lr, li = lb_re_ref[:, cst], lb_im_ref[:, cst]
        kr, ki = car_re_ref[:, cst], car_im_ref[:, cst]
        cr = cs[:, :slab_st] + (lr * kr - li * ki)
        ci = cs[:, slab_st:] + (lr * ki + li * kr)
        pr, pi = pp_re_ref[:, cst], pp_im_ref[:, cst]
        sr = cr * pr - ci * pi
        si = cr * pi + ci * pr
        car_re_ref[:, cst] = sr[t_len - 1:t_len, :]
        car_im_ref[:, cst] = si[t_len - 1:t_len, :]
        sb = jnp.concatenate([sr, si], axis=1).astype(BF16)
        y = jnp.dot(sb, cbd_ref[s], preferred_element_type=F32)
        y_ref[:, cin] = y + d_ref[:, cin] * u[:, cin]


def _ssm(u, bbd, cbd, pn_re, pn_im, pp_re, pp_im, lb_re, lb_im, d_skip, tri):
    n, w = u.shape
    t = SSM_CHUNK
    ns = pn_re.shape[1]
    const2 = lambda shape: pl.BlockSpec(shape, lambda i: (0, 0))
    const3 = lambda shape: pl.BlockSpec(shape, lambda i: (0, 0, 0))
    return pl.pallas_call(
        _ssm_kernel,
        grid=(n // t,),
        in_specs=[
            pl.BlockSpec((t, w), lambda i: (i, 0)),
            const3(bbd.shape), const3(cbd.shape),
            const2((t, ns)), const2((t, ns)), const2((t, ns)), const2((t, ns)),
            const2((1, ns)), const2((1, ns)), const2((1, w)), const2((t, t)),
        ],
        out_specs=pl.BlockSpec((t, w), lambda i: (i, 0)),
        out_shape=jax.ShapeDtypeStruct((n, w), F32),
        scratch_shapes=[pltpu.VMEM((1, ns), F32), pltpu.VMEM((1, ns), F32)],
        compiler_params=_params("arbitrary"),
        name="ssm_scan",
    )(u, bbd, cbd, pn_re, pn_im, pp_re, pp_im, lb_re, lb_im, d_skip, tri)


def _complex_powers(lr, li, count):
    pr, pi = jnp.ones_like(lr), jnp.zeros_like(li)
    cr, ci = lr, li
    rows = 1
    while rows < count:
        pr, pi = (jnp.concatenate([pr, pr * cr - pi * ci], axis=0),
                  jnp.concatenate([pi, pr * ci + pi * cr], axis=0))
        cr, ci = cr * cr - ci * ci, 2.0 * cr * ci
        rows *= 2
    return pr[:count], pi[:count]


def _ssm_tables(a_re, a_im, b_re, b_im, c_re, c_im, log_dt):
    g, p = a_re.shape
    h = b_re.shape[2]
    dt = jnp.exp(log_dt)[:, None]
    mag = jnp.exp(a_re * dt)
    lb_re = mag * jnp.cos(a_im * dt)
    lb_im = mag * jnp.sin(a_im * dt)
    nr, ni = lb_re - 1.0, lb_im
    den = a_re * a_re + a_im * a_im
    coef_re = (nr * a_re + ni * a_im) / den
    coef_im = (ni * a_re - nr * a_im) / den
    bb_re = coef_re[:, :, None] * b_re - coef_im[:, :, None] * b_im
    bb_im = coef_re[:, :, None] * b_im + coef_im[:, :, None] * b_re
    gs = g // SSM_SLABS
    eye = jnp.eye(gs, dtype=F32)

    def in_blocks(bb):
        t = bb.reshape(SSM_SLABS, gs, p, h).transpose(0, 1, 3, 2)
        return jnp.einsum("ab,sbhp->sahbp", eye, t).reshape(SSM_SLABS, gs * h, gs * p)

    def out_blocks(cc):
        t = cc.reshape(SSM_SLABS, gs, h, p).transpose(0, 1, 3, 2)
        return jnp.einsum("ab,sbph->sbpah", eye, t).reshape(SSM_SLABS, gs * p, gs * h)

    bbd = jnp.concatenate([in_blocks(bb_re), in_blocks(bb_im)], axis=2).astype(BF16)
    cbd = jnp.concatenate([out_blocks(c_re), -out_blocks(c_im)], axis=1).astype(BF16)
    lr, li = lb_re.reshape(1, g * p), lb_im.reshape(1, g * p)
    inv_den = lr * lr + li * li
    pp_re, pp_im = _complex_powers(lr, li, SSM_CHUNK)
    pn_re, pn_im = _complex_powers(lr / inv_den, -li / inv_den, SSM_CHUNK)
    return bbd, cbd, pn_re, pn_im, pp_re, pp_im, lr, li


def _attn_kernel(lam_ref, qt_ref, k_ref, vt_ref, g_ref, o_ref,
                 qz_ref, sa_ref, sb_ref, mxa_ref, mxb_ref, m_ref, acc_ref, *, tk, out_scale):
    i = pl.program_id(1)
    tq = qt_ref.shape[1]
    assert tq == tk
    qt = qt_ref[...]
    comp = lax.broadcasted_iota(jnp.int32, qt.shape, 0) < HEAD_DIM
    zero = jnp.zeros_like(qt)
    qz_ref[0] = jnp.where(comp, qt, zero)
    qz_ref[1] = jnp.where(comp, zero, qt)
    m_ref[...] = jnp.full_like(m_ref, NEG_INF)
    acc_ref[...] = jnp.zeros_like(acc_ref)

    def scores(j, s_ref, mx_ref, masked):
        start = pl.multiple_of(j * tk, tk)
        kb = k_ref[pl.ds(start, tk), :]
        for c in range(2):
            s = jnp.dot(kb, qz_ref[c], preferred_element_type=F32)
            if masked:
                key = lax.broadcasted_iota(jnp.int32, s.shape, 0)
                qry = lax.broadcasted_iota(jnp.int32, s.shape, 1)
                s = jnp.where(key <= qry, s, NEG_INF)
            s_ref[c] = s
            mx_ref[c] = jnp.max(s, axis=0, keepdims=True)

    def update(j, s_ref, mx_ref):
        start = pl.multiple_of(j * tk, tk)
        vtb = vt_ref[:, pl.ds(start, tk)]
        for c in range(2):
            m_prev = m_ref[c]
            m_new = jnp.maximum(m_prev, mx_ref[c])
            alpha = jnp.exp2(m_prev - m_new)
            p = jnp.exp2(s_ref[c] - m_new).astype(BF16)
            acc_ref[c] = alpha * acc_ref[c] + jnp.dot(vtb, p, preferred_element_type=F32)
            m_ref[c] = m_new

    @pl.when(i == 0)
    def _():
        scores(0, sa_ref, mxa_ref, True)
        update(0, sa_ref, mxa_ref)

    @pl.when(i > 0)
    def _():
        scores(0, sa_ref, mxa_ref, False)

    def pair(jj, carry):
        a = 2 * jj
        scores(a + 1, sb_ref, mxb_ref, False)
        update(a, sa_ref, mxa_ref)
        scores(a + 2, sa_ref, mxa_ref, False)
        update(a + 1, sb_ref, mxb_ref)
        return carry

    n_pairs = lax.shift_right_logical(jnp.maximum(i - 1, 0), 1)
    lax.fori_loop(0, n_pairs, pair, 0)
    first = 2 * n_pairs

    @pl.when(i % 2 == 1)
    def _():
        scores(i, sb_ref, mxb_ref, True)
        update(first, sa_ref, mxa_ref)
        update(i, sb_ref, mxb_ref)

    @pl.when((i % 2 == 0) & (i > 0))
    def _():
        scores(first + 1, sb_ref, mxb_ref, False)
        update(first, sa_ref, mxa_ref)
        scores(i, sa_ref, mxa_ref, True)
        update(first + 1, sb_ref, mxb_ref)
        update(i, sa_ref, mxa_ref)

    hw = 2 * HEAD_DIM
    o1 = acc_ref[0, :hw, :] / acc_ref[0, hw:hw + 1, :]
    o2 = acc_ref[1, :hw, :] / acc_ref[1, hw:hw + 1, :]
    ot = o1 - lam_ref[0, 0] * o2
    ms = jnp.mean(ot * ot, axis=0, keepdims=True)
    o = (ot * lax.rsqrt(ms + RMS_EPS)).T
    o_ref[...] = (o * g_ref[...] * out_scale).astype(o_ref.dtype)


def _attention(qt, k, vt, lam, subln_g, lam_init):
    n = k.shape[0]
    tq = min(ATTN_Q, n)
    tk = min(ATTN_K, n)
    hw = 2 * HEAD_DIM
    kernel = functools.partial(_attn_kernel, tk=tk, out_scale=1.0 - lam_init)
    return pl.pallas_call(
        kernel,
        grid=(N_HEADS, n // tq),
        in_specs=[
            pl.BlockSpec(memory_space=pltpu.SMEM),
            pl.BlockSpec((hw, tq), lambda h, i: (h, i)),
            pl.BlockSpec((n, hw), lambda h, i: (0, h)),
            pl.BlockSpec((VT_ROWS, n), lambda h, i: (h, 0)),
            pl.BlockSpec((1, hw), lambda h, i: (0, 0)),
        ],
        out_specs=pl.BlockSpec((tq, hw), lambda h, i: (i, h)),
        out_shape=jax.ShapeDtypeStruct((n, V_WIDTH), BF16),
        scratch_shapes=[
            pltpu.VMEM((2, hw, tq), BF16),
            pltpu.VMEM((2, tk, tq), F32), pltpu.VMEM((2, tk, tq), F32),
            pltpu.VMEM((2, 1, tq), F32), pltpu.VMEM((2, 1, tq), F32),
            pltpu.VMEM((2, 1, tq), F32),
            pltpu.VMEM((2, VT_ROWS, tq), F32),
        ],
        compiler_params=_params("parallel", "arbitrary"),
        name="diff_attn",
    )(lam, qt, k, vt, subln_g)


def _mix_kernel(x_ref, ys_ref, at_ref, sg_ref, wglu_ref, wbs_ref, wba_ref, wout_ref,
                g_ref, b_ref, o_ref):
    d = x_ref.shape[1]
    y = ys_ref[...]
    z = 0.5 * y * (1.0 + lax.erf(y * (2.0 ** -0.5)))
    gate = _sigmoid(jnp.dot(z.astype(BF16), wglu_ref[...], preferred_element_type=F32))
    glu = (z * gate).astype(BF16)
    ya = jnp.dot(glu, wbs_ref[...], preferred_element_type=F32)
    yb = jnp.dot(at_ref[...], wba_ref[...], preferred_element_type=F32)
    sg = sg_ref[...].astype(F32)
    mix = (sg[:, :d] * ya + sg[:, d:] * yb).astype(BF16)
    out = ALPHA * x_ref[...] + jnp.dot(mix, wout_ref[...], preferred_element_type=F32)
    o_ref[...] = _layer_norm(out, g_ref[...], b_ref[...])


def _resident(shape):
    return pl.BlockSpec(shape, lambda i: (0,) * len(shape), pipeline_mode=pl.Buffered(1))


def _mix(x, ys, attn, sg, wglu, wbs, wba, wout, g, b):
    n, d = x.shape
    tm = min(MIX_ROWS, n)
    rows = lambda w: pl.BlockSpec((tm, w), lambda i: (i, 0))
    return pl.pallas_call(
        _mix_kernel,
        grid=(n // tm,),
        in_specs=[
            rows(d), rows(ys.shape[1]), rows(attn.shape[1]), rows(sg.shape[1]),
            _resident(wglu.shape), _resident(wbs.shape), _resident(wba.shape), _resident(wout.shape),
            _resident((1, d)), _resident((1, d)),
        ],
        out_specs=rows(d),
        out_shape=jax.ShapeDtypeStruct((n, d), F32),
        compiler_params=_params("parallel"),
        name="mixer_out",
    )(x, ys, attn, sg, wglu, wbs, wba, wout, g, b)


def _ple_kernel(x_ref, p_ref, wg_ref, wp_ref, g_ref, b_ref, o_ref):
    x = x_ref[...]
    gate = _sigmoid(jnp.dot(x.astype(BF16), wg_ref[...], preferred_element_type=F32))
    emb = jnp.dot(p_ref[...].astype(BF16), wp_ref[...], preferred_element_type=F32)
    o_ref[...] = _layer_norm(ALPHA * x + gate * emb, g_ref[...], b_ref[...])


def _ple(x, p, wg, wp, g, b):
    n, d = x.shape
    tm = min(PLE_ROWS, n)
    rows = lambda w: pl.BlockSpec((tm, w), lambda i: (i, 0))
    return pl.pallas_call(
        _ple_kernel,
        grid=(n // tm,),
        in_specs=[rows(d), rows(p.shape[1]), _resident(wg.shape), _resident(wp.shape),
                  _resident((1, d)), _resident((1, d))],
        out_specs=rows(d),
        out_shape=jax.ShapeDtypeStruct((n, d), F32),
        compiler_params=_params("parallel"),
        name="ple_ln",
    )(x, p, wg, wp, g, b)


def _rope_tables(positions):
    half = ROT_DIM // 2
    inv_freq = ROPE_THETA ** (-jnp.arange(0, ROT_DIM, 2, dtype=F32) / ROT_DIM)
    ang = positions.astype(F32)[:, None] * inv_freq
    cos, sin = jnp.cos(ang), jnp.sin(ang)
    n = positions.shape[0]
    pad = jnp.zeros((n, HEAD_DIM - ROT_DIM), F32)
    zeros = jnp.zeros((n, half), F32)
    cos_t = jnp.concatenate([cos, cos, pad + 1.0], axis=1)
    sin_lo = jnp.concatenate([-sin, zeros, pad], axis=1)
    sin_hi = jnp.concatenate([zeros, sin, pad], axis=1)
    reps = LANES // HEAD_DIM
    return tuple(jnp.tile(t, (1, reps)) for t in (cos_t, sin_lo, sin_hi))


def kernel(x, p, positions, ffn1_w_gate, ffn1_w_up, ffn1_w_down, ln1_g, ln1_b, w_in, ssm_a_re, ssm_a_im, ssm_b_re, ssm_b_im, ssm_c_re, ssm_c_im, ssm_log_dt, ssm_d, ssm_w_glu, w_branch_ssm, lambda_q1, lambda_k1, lambda_q2, lambda_k2, attn_subln_g, w_branch_attn, w_out, ln2_g, ln2_b, ffn2_w_gate, ffn2_w_up, ffn2_w_down, ln3_g, ln3_b, ple_w_gate, ple_w_proj, ln4_g, ln4_b):
    bsz, length, d = x.shape
    assert bsz == 1
    xs = x.reshape(length, d)
    cos, slo, shi = _rope_tables(positions[0])
    tri = jnp.tril(jnp.ones((SSM_CHUNK, SSM_CHUNK), F32)).astype(BF16)
    row = lambda a: a.reshape(1, -1)
    for i in range(DEPTH):
        lam_init = 0.8 - 0.6 * math.exp(-0.3 * i)
        lam = (jnp.exp(jnp.sum(lambda_q1[i] * lambda_k1[i])) - jnp.exp(jnp.sum(lambda_q2[i] * lambda_k2[i]))
               + lam_init).reshape(1, 1)
        xs = _ffn_ln(xs, ffn1_w_gate[i].astype(BF16), ffn1_w_up[i].astype(BF16),
                     (0.5 * ffn1_w_down[i]).astype(BF16), row(ln1_g[i]), row(ln1_b[i]))
        u, qt, k, vt, sg = _inproj(xs, w_in[i].astype(BF16), cos, slo, shi)
        tables = _ssm_tables(ssm_a_re[i], ssm_a_im[i], ssm_b_re[i], ssm_b_im[i],
                             ssm_c_re[i], ssm_c_im[i], ssm_log_dt[i])
        ys = _ssm(u, *tables, row(ssm_d[i]), tri)
        attn = _attention(qt, k, vt, lam, row(attn_subln_g[i]), lam_init)
        xs = _mix(xs, ys, attn, sg, ssm_w_glu[i].astype(BF16), w_branch_ssm[i].astype(BF16),
                  w_branch_attn[i].astype(BF16), w_out[i].astype(BF16), row(ln2_g[i]), row(ln2_b[i]))
        xs = _ffn_ln(xs, ffn2_w_gate[i].astype(BF16), ffn2_w_up[i].astype(BF16),
                     (0.5 * ffn2_w_down[i]).astype(BF16), row(ln3_g[i]), row(ln3_b[i]))
        xs = _ple(xs, p[i, 0], ple_w_gate[i].astype(BF16), ple_w_proj[i].astype(BF16),
                  row(ln4_g[i]), row(ln4_b[i]))
    return xs.reshape(bsz, length, d)
```

```python
import functools
import math

import jax
import jax.numpy as jnp
from jax import lax
from jax.experimental import pallas as pl
from jax.experimental.pallas import tpu as pltpu

F32 = jnp.float32
BF16 = jnp.bfloat16

DEPTH = 2
SSM_WIDTH = 1024
SSM_GROUP = 16
SSM_GROUPS = SSM_WIDTH // SSM_GROUP
SSM_STATE = 64
N_HEADS = 8
HEAD_DIM = 64
QK_WIDTH = N_HEADS * 2 * HEAD_DIM
V_WIDTH = N_HEADS * 2 * HEAD_DIM
ROT_DIM = HEAD_DIM // 4
ROPE_THETA = 500000.0
LN_EPS = 1e-5
RMS_EPS = 1e-5
NEG_INF = -1e30
ALPHA = (2 * DEPTH) ** 0.25
Q_SCALE = HEAD_DIM ** -0.5 * math.log2(math.e)

LANES = 128
BF16_ROWS = 16
VT_ROWS = 2 * HEAD_DIM + BF16_ROWS
VMEM_LIMIT_BYTES = 56 * 1024 * 1024

FFN_ROWS = 512
FFN_COLS = 512
PROJ_ROWS = 1024
PROJ_COLS = 1024
PROJ_CHUNK = 256
SSM_CHUNK = 128
SSM_ROWS = 256
SSM_SLABS = 4
ATTN_Q = 512
ATTN_K = 512
MIX_ROWS = 256
PLE_ROWS = 512


def _sigmoid(x):
    return 1.0 / (1.0 + jnp.exp(-x))


def _layer_norm(y, g, b):
    mu = jnp.mean(y, axis=-1, keepdims=True)
    yc = y - mu
    var = jnp.mean(yc * yc, axis=-1, keepdims=True)
    return yc * lax.rsqrt(var + LN_EPS) * g + b


def _params(*semantics):
    return pltpu.CompilerParams(dimension_semantics=semantics, vmem_limit_bytes=VMEM_LIMIT_BYTES)


def _ffn_ln_kernel(x_ref, wg_ref, wu_ref, wd_ref, g_ref, b_ref, o_ref, ob_ref, xb_ref, acc_ref):
    j = pl.program_id(1)

    @pl.when(j == 0)
    def _():
        xb_ref[...] = x_ref[...].astype(BF16)
        acc_ref[...] = jnp.zeros_like(acc_ref)

    xb = xb_ref[...]
    hg = jnp.dot(xb, wg_ref[...], preferred_element_type=F32)
    hu = jnp.dot(xb, wu_ref[...], preferred_element_type=F32)
    act = (hg * _sigmoid(hg) * hu).astype(BF16)
    acc_ref[...] += jnp.dot(act, wd_ref[...], preferred_element_type=F32)

    @pl.when(j == pl.num_programs(1) - 1)
    def _():
        y = _layer_norm(ALPHA * x_ref[...] + acc_ref[...], g_ref[...], b_ref[...])
        o_ref[...] = y
        ob_ref[...] = y.astype(BF16)


def _ffn_ln(x, wg, wu, wd, g, b):
    n, d = x.shape
    f = wg.shape[1]
    tm = min(FFN_ROWS, n)
    tf = FFN_COLS
    return pl.pallas_call(
        _ffn_ln_kernel,
        grid=(n // tm, f // tf),
        in_specs=[
            pl.BlockSpec((tm, d), lambda i, j: (i, 0)),
            pl.BlockSpec((d, tf), lambda i, j: (0, j)),
            pl.BlockSpec((d, tf), lambda i, j: (0, j)),
            pl.BlockSpec((tf, d), lambda i, j: (j, 0)),
            pl.BlockSpec((1, d), lambda i, j: (0, 0)),
            pl.BlockSpec((1, d), lambda i, j: (0, 0)),
        ],
        out_specs=[pl.BlockSpec((tm, d), lambda i, j: (i, 0)), pl.BlockSpec((tm, d), lambda i, j: (i, 0))],
        out_shape=[jax.ShapeDtypeStruct((n, d), F32), jax.ShapeDtypeStruct((n, d), BF16)],
        scratch_shapes=[pltpu.VMEM((tm, d), BF16), pltpu.VMEM((tm, d), F32)],
        compiler_params=_params("parallel", "arbitrary"),
        name="ffn_ln",
    )(x, wg, wu, wd, g, b)


def _rotary(h, cos, sin_lo, sin_hi):
    half = ROT_DIM // 2
    return (h * cos + pltpu.roll(h, half, 1) * sin_hi
            + pltpu.roll(h, LANES - half, 1) * sin_lo)


def _inproj_kernel(xb_ref, w_ref, cos_ref, slo_ref, shi_ref, u_ref, qt_ref, k_ref, vt_ref, g_ref):
    j = pl.program_id(1)

    def for_chunks(store):
        xb = xb_ref[...]
        for c in range(w_ref.shape[1] // PROJ_CHUNK):
            cols = slice(c * PROJ_CHUNK, (c + 1) * PROJ_CHUNK)
            h = jnp.dot(xb, w_ref[:, cols], preferred_element_type=F32)
            for s in range(PROJ_CHUNK // LANES):
                lo = c * PROJ_CHUNK + s * LANES
                store(slice(lo, lo + LANES), h[:, s * LANES:(s + 1) * LANES])

    def rope(h):
        return _rotary(h, cos_ref[...], slo_ref[...], shi_ref[...])

    @pl.when(j == 0)
    def _():
        def store(cols, h):
            u_ref[:, cols] = h
        for_chunks(store)

    @pl.when(j == 1)
    def _():
        def store(cols, h):
            qt_ref[cols, :] = (rope(h) * Q_SCALE).T.astype(BF16)
        for_chunks(store)

    @pl.when(j == 2)
    def _():
        def store(cols, h):
            k_ref[:, cols] = rope(h).astype(BF16)
        for_chunks(store)

    @pl.when(j == 3)
    def _():
        def store(cols, h):
            base = cols.start // LANES * VT_ROWS
            vt_ref[base:base + LANES, :] = h.T.astype(BF16)
            vt_ref[base + LANES:base + VT_ROWS, :] = jnp.ones((VT_ROWS - LANES, h.shape[0]), BF16)
        for_chunks(store)

    @pl.when(j >= 4)
    def _():
        def store(cols, h):
            g_ref[:, cols] = _sigmoid(h).astype(BF16)
        for_chunks(store)


def _inproj(x, w, cos, slo, shi):
    n, d = x.shape
    tm = min(PROJ_ROWS, n)
    tn = PROJ_COLS
    width = w.shape[1]
    n_gate = (width - SSM_WIDTH - 2 * QK_WIDTH - V_WIDTH) // tn
    assert SSM_WIDTH == tn and QK_WIDTH == tn and V_WIDTH == tn
    tab = pl.BlockSpec((tm, LANES), lambda i, j: (i, 0))
    return pl.pallas_call(
        _inproj_kernel,
        grid=(n // tm, width // tn),
        in_specs=[
            pl.BlockSpec((tm, d), lambda i, j: (i, 0)),
            pl.BlockSpec((d, tn), lambda i, j: (0, j)),
            tab, tab, tab,
        ],
        out_specs=[
            pl.BlockSpec((tm, tn), lambda i, j: (i, 0)),
            pl.BlockSpec((tn, tm), lambda i, j: (0, i)),
            pl.BlockSpec((tm, tn), lambda i, j: (i, 0)),
            pl.BlockSpec((N_HEADS * VT_ROWS, tm), lambda i, j: (0, i)),
            pl.BlockSpec((tm, tn), lambda i, j: (i, jnp.clip(j - 4, 0, n_gate - 1))),
        ],
        out_shape=[
            jax.ShapeDtypeStruct((n, SSM_WIDTH), F32),
            jax.ShapeDtypeStruct((QK_WIDTH, n), BF16),
            jax.ShapeDtypeStruct((n, QK_WIDTH), BF16),
            jax.ShapeDtypeStruct((N_HEADS * VT_ROWS, n), BF16),
            jax.ShapeDtypeStruct((n, n_gate * tn), BF16),
        ],
        compiler_params=_params("parallel", "arbitrary"),
        name="inproj",
    )(x, w, cos, slo, shi)


def _ssm_kernel(u_ref, bbd_ref, cbd_ref, pn_re_ref, pn_im_ref, pp_re_ref, pp_im_ref,
                lb_re_ref, lb_im_ref, d_ref, tri_ref, y_ref, car_re_ref, car_im_ref):
    @pl.when(pl.program_id(0) == 0)
    def _():
        car_re_ref[...] = jnp.zeros_like(car_re_ref)
        car_im_ref[...] = jnp.zeros_like(car_im_ref)

    t_len = pn_re_ref.shape[0]
    n_sub = u_ref.shape[0] // t_len
    slab_in = u_ref.shape[1] // SSM_SLABS
    slab_st = pn_re_ref.shape[1] // SSM_SLABS
    u = u_ref[...]
    ub = u.astype(BF16)
    tri = tri_ref[...]
    for s in range(SSM_SLABS):
        cin = slice(s * slab_in, (s + 1) * slab_in)
        cst = slice(s * slab_st, (s + 1) * slab_st)
        bu = jnp.dot(ub[:, cin], bbd_ref[s], preferred_element_type=F32)
        nr, ni = pn_re_ref[:, cst], pn_im_ref[:, cst]
        z = []
        for k in range(n_sub):
            br = bu[k * t_len:(k + 1) * t_len, :slab_st]
            bi = bu[k * t_len:(k + 1) * t_len, slab_st:]
            z.append(jnp.concatenate([br * nr - bi * ni, br * ni + bi * nr], axis=1).astype(BF16))
        cs = jnp.dot(tri, jnp.concatenate(z, axis=0), preferred_element_type=F32)
        lr, li = lb_re_ref[:, cst], lb_im_ref[:, cst]
        pr, pi = pp_re_ref[:, cst], pp_im_ref[:, cst]
        kr, ki = car_re_ref[:, cst], car_im_ref[:, cst]
        states = []
        for k in range(n_sub):
            cr = cs[k * t_len:(k + 1) * t_len, :slab_st] + (lr * kr - li * ki)
            ci = cs[k * t_len:(k + 1) * t_len, slab_st:] + (lr * ki + li * kr)
            sr = cr * pr - ci * pi
            si = cr * pi + ci * pr
            kr, ki = sr[t_len - 1:t_len, :], si[t_len - 1:t_len, :]
            states.append(jnp.concatenate([sr, si], axis=1).astype(BF16))
        car_re_ref[:, cst] = kr
        car_im_ref[:, cst] = ki
        y = jnp.dot(jnp.concatenate(states, axis=0), cbd_ref[s], preferred_element_type=F32)
        y_ref[:, cin] = y + d_ref[:, cin] * u[:, cin]


def _ssm(u, bbd, cbd, pn_re, pn_im, pp_re, pp_im, lb_re, lb_im, d_skip, tri):
    n, w = u.shape
    t = SSM_CHUNK
    rows = tri.shape[0]
    ns = pn_re.shape[1]
    const2 = lambda shape: pl.BlockSpec(shape, lambda i: (0, 0))
    const3 = lambda shape: pl.BlockSpec(shape, lambda i: (0, 0, 0))
    return pl.pallas_call(
        _ssm_kernel,
        grid=(n // rows,),
        in_specs=[
            pl.BlockSpec((rows, w), lambda i: (i, 0)),
            const3(bbd.shape), const3(cbd.shape),
            const2((t, ns)), const2((t, ns)), const2((t, ns)), const2((t, ns)),
            const2((1, ns)), const2((1, ns)), const2((1, w)), const2((rows, rows)),
        ],
        out_specs=pl.BlockSpec((rows, w), lambda i: (i, 0)),
        out_shape=jax.ShapeDtypeStruct((n, w), F32),
        scratch_shapes=[pltpu.VMEM((1, ns), F32), pltpu.VMEM((1, ns), F32)],
        compiler_params=_params("arbitrary"),
        name="ssm_scan",
    )(u, bbd, cbd, pn_re, pn_im, pp_re, pp_im, lb_re, lb_im, d_skip, tri)


def _complex_powers(lr, li, count):
    pr, pi = jnp.ones_like(lr), jnp.zeros_like(li)
    cr, ci = lr, li
    rows = 1
    while rows < count:
        pr, pi = (jnp.concatenate([pr, pr * cr - pi * ci], axis=0),
                  jnp.concatenate([pi, pr * ci + pi * cr], axis=0))
        cr, ci = cr * cr - ci * ci, 2.0 * cr * ci
        rows *= 2
    return pr[:count], pi[:count]


def _ssm_tables(a_re, a_im, b_re, b_im, c_re, c_im, log_dt):
    g, p = a_re.shape
    h = b_re.shape[2]
    dt = jnp.exp(log_dt)[:, None]
    mag = jnp.exp(a_re * dt)
    lb_re = mag * jnp.cos(a_im * dt)
    lb_im = mag * jnp.sin(a_im * dt)
    nr, ni = lb_re - 1.0, lb_im
    den = a_re * a_re + a_im * a_im
    coef_re = (nr * a_re + ni * a_im) / den
    coef_im = (ni * a_re - nr * a_im) / den
    bb_re = coef_re[:, :, None] * b_re - coef_im[:, :, None] * b_im
    bb_im = coef_re[:, :, None] * b_im + coef_im[:, :, None] * b_re
    gs = g // SSM_SLABS
    eye = jnp.eye(gs, dtype=F32)

    def in_blocks(bb):
        t = bb.reshape(SSM_SLABS, gs, p, h).transpose(0, 1, 3, 2)
        return jnp.einsum("ab,sbhp->sahbp", eye, t).reshape(SSM_SLABS, gs * h, gs * p)

    def out_blocks(cc):
        t = cc.reshape(SSM_SLABS, gs, h, p).transpose(0, 1, 3, 2)
        return jnp.einsum("ab,sbph->sbpah", eye, t).reshape(SSM_SLABS, gs * p, gs * h)

    bbd = jnp.concatenate([in_blocks(bb_re), in_blocks(bb_im)], axis=2).astype(BF16)
    cbd = jnp.concatenate([out_blocks(c_re), -out_blocks(c_im)], axis=1).astype(BF16)
    lr, li = lb_re.reshape(1, g * p), lb_im.reshape(1, g * p)
    inv_den = lr * lr + li * li
    pp_re, pp_im = _complex_powers(lr, li, SSM_CHUNK)
    pn_re, pn_im = _complex_powers(lr / inv_den, -li / inv_den, SSM_CHUNK)
    return bbd, cbd, pn_re, pn_im, pp_re, pp_im, lr, li


def _attn_kernel(lam_ref, qt_ref, k_ref, vt_ref, g_ref, o_ref,
                 qz_ref, sa_ref, sb_ref, mxa_ref, mxb_ref, m_ref, acc_ref, *, tk, out_scale):
    i = pl.program_id(1)
    tq = qt_ref.shape[1]
    assert tq == tk
    qt = qt_ref[...]
    comp = lax.broadcasted_iota(jnp.int32, qt.shape, 0) < HEAD_DIM
    zero = jnp.zeros_like(qt)
    qz_ref[0] = jnp.where(comp, qt, zero)
    qz_ref[1] = jnp.where(comp, zero, qt)
    m_ref[...] = jnp.full_like(m_ref, NEG_INF)
    acc_ref[...] = jnp.zeros_like(acc_ref)

    def scores(j, s_ref, mx_ref, masked):
        start = pl.multiple_of(j * tk, tk)
        kb = k_ref[pl.ds(start, tk), :]
        for c in range(2):
            s = jnp.dot(kb, qz_ref[c], preferred_element_type=F32)
            if masked:
                key = lax.broadcasted_iota(jnp.int32, s.shape, 0)
                qry = lax.broadcasted_iota(jnp.int32, s.shape, 1)
                s = jnp.where(key <= qry, s, NEG_INF)
            s_ref[c] = s
            mx_ref[c] = jnp.max(s, axis=0, keepdims=True)

    def update(j, s_ref, mx_ref):
        start = pl.multiple_of(j * tk, tk)
        vtb = vt_ref[:, pl.ds(start, tk)]
        for c in range(2):
            m_prev = m_ref[c]
            m_new = jnp.maximum(m_prev, mx_ref[c])
            alpha = jnp.exp2(m_prev - m_new)
            p = jnp.exp2(s_ref[c] - m_new).astype(BF16)
            acc_ref[c] = alpha * acc_ref[c] + jnp.dot(vtb, p, preferred_element_type=F32)
            m_ref[c] = m_new

    @pl.when(i == 0)
    def _():
        scores(0, sa_ref, mxa_ref, True)
        update(0, sa_ref, mxa_ref)

    @pl.when(i > 0)
    def _():
        scores(0, sa_ref, mxa_ref, False)

    def pair(jj, carry):
        a = 2 * jj
        scores(a + 1, sb_ref, mxb_ref, False)
        update(a, sa_ref, mxa_ref)
        scores(a + 2, sa_ref, mxa_ref, False)
        update(a + 1, sb_ref, mxb_ref)
        return carry

    n_pairs = lax.shift_right_logical(jnp.maximum(i - 1, 0), 1)
    lax.fori_loop(0, n_pairs, pair, 0)
    first = 2 * n_pairs

    @pl.when(i % 2 == 1)
    def _():
        scores(i, sb_ref, mxb_ref, True)
        update(first, sa_ref, mxa_ref)
        update(i, sb_ref, mxb_ref)

    @pl.when((i % 2 == 0) & (i > 0))
    def _():
        scores(first + 1, sb_ref, mxb_ref, False)
        update(first, sa_ref, mxa_ref)
        scores(i, sa_ref, mxa_ref, True)
        update(first + 1, sb_ref, mxb_ref)
        update(i, sa_ref, mxa_ref)

    hw = 2 * HEAD_DIM
    o1 = acc_ref[0, :hw, :] / acc_ref[0, hw:hw + 1, :]
    o2 = acc_ref[1, :hw, :] / acc_ref[1, hw:hw + 1, :]
    ot = o1 - lam_ref[0, 0] * o2
    ms = jnp.mean(ot * ot, axis=0, keepdims=True)
    o = (ot * lax.rsqrt(ms + RMS_EPS)).T
    o_ref[...] = (o * g_ref[...] * out_scale).astype(o_ref.dtype)


def _attention(qt, k, vt, lam, subln_g, lam_init):
    n = k.shape[0]
    tq = min(ATTN_Q, n)
    tk = min(ATTN_K, n)
    hw = 2 * HEAD_DIM
    kernel = functools.partial(_attn_kernel, tk=tk, out_scale=1.0 - lam_init)
    return pl.pallas_call(
        kernel,
        grid=(N_HEADS, n // tq),
        in_specs=[
            pl.BlockSpec(memory_space=pltpu.SMEM),
            pl.BlockSpec((hw, tq), lambda h, i: (h, i)),
            pl.BlockSpec((n, hw), lambda h, i: (0, h)),
            pl.BlockSpec((VT_ROWS, n), lambda h, i: (h, 0)),
            pl.BlockSpec((1, hw), lambda h, i: (0, 0)),
        ],
        out_specs=pl.BlockSpec((tq, hw), lambda h, i: (i, h)),
        out_shape=jax.ShapeDtypeStruct((n, V_WIDTH), BF16),
        scratch_shapes=[
            pltpu.VMEM((2, hw, tq), BF16),
            pltpu.VMEM((2, tk, tq), F32), pltpu.VMEM((2, tk, tq), F32),
            pltpu.VMEM((2, 1, tq), F32), pltpu.VMEM((2, 1, tq), F32),
            pltpu.VMEM((2, 1, tq), F32),
            pltpu.VMEM((2, VT_ROWS, tq), F32),
        ],
        compiler_params=_params("parallel", "arbitrary"),
        name="diff_attn",
    )(lam, qt, k, vt, subln_g)


def _mix_kernel(x_ref, ys_ref, at_ref, sg_ref, wglu_ref, wbs_ref, wba_ref, wout_ref,
                g_ref, b_ref, o_ref):
    d = x_ref.shape[1]
    y = ys_ref[...]
    z = 0.5 * y * (1.0 + lax.erf(y * (2.0 ** -0.5)))
    gate = _sigmoid(jnp.dot(z.astype(BF16), wglu_ref[...], preferred_element_type=F32))
    glu = (z * gate).astype(BF16)
    ya = jnp.dot(glu, wbs_ref[...], preferred_element_type=F32)
    yb = jnp.dot(at_ref[...], wba_ref[...], preferred_element_type=F32)
    sg = sg_ref[...].astype(F32)
    mix = (sg[:, :d] * ya + sg[:, d:] * yb).astype(BF16)
    out = ALPHA * x_ref[...] + jnp.dot(mix, wout_ref[...], preferred_element_type=F32)
    o_ref[...] = _layer_norm(out, g_ref[...], b_ref[...])


def _resident(shape):
    return pl.BlockSpec(shape, lambda i: (0,) * len(shape), pipeline_mode=pl.Buffered(1))


def _mix(x, ys, attn, sg, wglu, wbs, wba, wout, g, b):
    n, d = x.shape
    tm = min(MIX_ROWS, n)
    rows = lambda w: pl.BlockSpec((tm, w), lambda i: (i, 0))
    return pl.pallas_call(
        _mix_kernel,
        grid=(n // tm,),
        in_specs=[
            rows(d), rows(ys.shape[1]), rows(attn.shape[1]), rows(sg.shape[1]),
            _resident(wglu.shape), _resident(wbs.shape), _resident(wba.shape), _resident(wout.shape),
            _resident((1, d)), _resident((1, d)),
        ],
        out_specs=rows(d),
        out_shape=jax.ShapeDtypeStruct((n, d), F32),
        compiler_params=_params("parallel"),
        name="mixer_out",
    )(x, ys, attn, sg, wglu, wbs, wba, wout, g, b)


def _ple_kernel(x_ref, xb_ref, p_ref, wg_ref, wp_ref, g_ref, b_ref, o_ref):
    x = x_ref[...]
    gate = _sigmoid(jnp.dot(xb_ref[...], wg_ref[...], preferred_element_type=F32))
    emb = jnp.dot(p_ref[...].astype(BF16), wp_ref[...], preferred_element_type=F32)
    o_ref[...] = _layer_norm(ALPHA * x + gate * emb, g_ref[...], b_ref[...])


def _ple(x, xb, p, wg, wp, g, b):
    n, d = x.shape
    tm = min(PLE_ROWS, n)
    rows = lambda w: pl.BlockSpec((tm, w), lambda i: (i, 0))
    return pl.pallas_call(
        _ple_kernel,
        grid=(n // tm,),
        in_specs=[rows(d), rows(d), rows(p.shape[1]), _resident(wg.shape), _resident(wp.shape),
                  _resident((1, d)), _resident((1, d))],
        out_specs=rows(d),
        out_shape=jax.ShapeDtypeStruct((n, d), F32),
        compiler_params=_params("parallel"),
        name="ple_ln",
    )(x, xb, p, wg, wp, g, b)


def _rope_tables(positions):
    half = ROT_DIM // 2
    inv_freq = ROPE_THETA ** (-jnp.arange(0, ROT_DIM, 2, dtype=F32) / ROT_DIM)
    ang = positions.astype(F32)[:, None] * inv_freq
    cos, sin = jnp.cos(ang), jnp.sin(ang)
    n = positions.shape[0]
    pad = jnp.zeros((n, HEAD_DIM - ROT_DIM), F32)
    zeros = jnp.zeros((n, half), F32)
    cos_t = jnp.concatenate([cos, cos, pad + 1.0], axis=1)
    sin_lo = jnp.concatenate([-sin, zeros, pad], axis=1)
    sin_hi = jnp.concatenate([zeros, sin, pad], axis=1)
    reps = LANES // HEAD_DIM
    return tuple(jnp.tile(t, (1, reps)) for t in (cos_t, sin_lo, sin_hi))


def kernel(x, p, positions, ffn1_w_gate, ffn1_w_up, ffn1_w_down, ln1_g, ln1_b, w_in, ssm_a_re, ssm_a_im, ssm_b_re, ssm_b_im, ssm_c_re, ssm_c_im, ssm_log_dt, ssm_d, ssm_w_glu, w_branch_ssm, lambda_q1, lambda_k1, lambda_q2, lambda_k2, attn_subln_g, w_branch_attn, w_out, ln2_g, ln2_b, ffn2_w_gate, ffn2_w_up, ffn2_w_down, ln3_g, ln3_b, ple_w_gate, ple_w_proj, ln4_g, ln4_b):
    bsz, length, d = x.shape
    assert bsz == 1
    xs = x.reshape(length, d)
    cos, slo, shi = _rope_tables(positions[0])
    tri = jnp.kron(jnp.eye(SSM_ROWS // SSM_CHUNK, dtype=F32),
                   jnp.tril(jnp.ones((SSM_CHUNK, SSM_CHUNK), F32))).astype(BF16)
    row = lambda a: a.reshape(1, -1)
    for i in range(DEPTH):
        lam_init = 0.8 - 0.6 * math.exp(-0.3 * i)
        lam = (jnp.exp(jnp.sum(lambda_q1[i] * lambda_k1[i])) - jnp.exp(jnp.sum(lambda_q2[i] * lambda_k2[i]))
               + lam_init).reshape(1, 1)
        xs, xb = _ffn_ln(xs, ffn1_w_gate[i].astype(BF16), ffn1_w_up[i].astype(BF16),
                         (0.5 * ffn1_w_down[i]).astype(BF16), row(ln1_g[i]), row(ln1_b[i]))
        u, qt, k, vt, sg = _inproj(xb, w_in[i].astype(BF16), cos, slo, shi)
        tables = _ssm_tables(ssm_a_re[i], ssm_a_im[i], ssm_b_re[i], ssm_b_im[i],
                             ssm_c_re[i], ssm_c_im[i], ssm_log_dt[i])
        ys = _ssm(u, *tables, row(ssm_d[i]), tri)
        attn = _attention(qt, k, vt, lam, row(attn_subln_g[i]), lam_init)
        xs = _mix(xs, ys, attn, sg, ssm_w_glu[i].astype(BF16), w_branch_ssm[i].astype(BF16),
                  w_branch_attn[i].astype(BF16), w_out[i].astype(BF16), row(ln2_g[i]), row(ln2_b[i]))
        xs, xb = _ffn_ln(xs, ffn2_w_gate[i].astype(BF16), ffn2_w_up[i].astype(BF16),
                         (0.5 * ffn2_w_down[i]).astype(BF16), row(ln3_g[i]), row(ln3_b[i]))
        xs = _ple(xs, xb, p[i, 0], ple_w_gate[i].astype(BF16), ple_w_proj[i].astype(BF16),
                  row(ln4_g[i]), row(ln4_b[i]))
    return xs.reshape(bsz, length, d)
```

```python
import functools
import math

import jax
import jax.numpy as jnp
from jax import lax
from jax.experimental import pallas as pl
from jax.experimental.pallas import tpu as pltpu

F32 = jnp.float32
BF16 = jnp.bfloat16

DEPTH = 2
SSM_WIDTH = 1024
SSM_GROUP = 16
SSM_GROUPS = SSM_WIDTH // SSM_GROUP
SSM_STATE = 64
N_HEADS = 8
HEAD_DIM = 64
QK_WIDTH = N_HEADS * 2 * HEAD_DIM
V_WIDTH = N_HEADS * 2 * HEAD_DIM
ROT_DIM = HEAD_DIM // 4
ROPE_THETA = 500000.0
LN_EPS = 1e-5
RMS_EPS = 1e-5
NEG_INF = -1e30
ALPHA = (2 * DEPTH) ** 0.25
Q_SCALE = HEAD_DIM ** -0.5 * math.log2(math.e)

LANES = 128
BF16_ROWS = 16
VT_ROWS = 2 * HEAD_DIM + BF16_ROWS
VMEM_LIMIT_BYTES = 56 * 1024 * 1024

FFN_ROWS = 512
FFN_COLS = 512
PROJ_ROWS = 1024
PROJ_COLS = 1024
PROJ_CHUNK = 256
SSM_CHUNK = 128
SSM_ROWS = 256
SSM_SLABS = 4
ATTN_Q = 1024
ATTN_K = 1024
MIX_ROWS = 256
PLE_ROWS = 512


def _sigmoid(x):
    return 1.0 / (1.0 + jnp.exp(-x))


def _layer_norm(y, g, b):
    mu = jnp.mean(y, axis=-1, keepdims=True)
    yc = y - mu
    var = jnp.mean(yc * yc, axis=-1, keepdims=True)
    return yc * lax.rsqrt(var + LN_EPS) * g + b


def _params(*semantics):
    return pltpu.CompilerParams(dimension_semantics=semantics, vmem_limit_bytes=VMEM_LIMIT_BYTES)


def _ffn_ln_kernel(x_ref, wg_ref, wu_ref, wd_ref, g_ref, b_ref, *rest, emit_bf16):
    if emit_bf16:
        o_ref, ob_ref, xb_ref, acc_ref, act_ref = rest
    else:
        o_ref, xb_ref, acc_ref, act_ref = rest
    i, j = pl.program_id(0), pl.program_id(1)
    n_rows = pl.num_programs(0) - 1
    n_chunks = pl.num_programs(1) - 1
    cur = i % 2

    def down():
        return jnp.dot(act_ref[(j + 1) % 2], wd_ref[...], preferred_element_type=F32)

    def gate_up():
        xb = xb_ref[...]
        hg = jnp.dot(xb, wg_ref[...], preferred_element_type=F32)
        hu = jnp.dot(xb, wu_ref[...], preferred_element_type=F32)
        act_ref[j % 2] = (hg * _sigmoid(hg) * hu).astype(BF16)

    @pl.when((i == 0) & (j == 0))
    def _():
        acc_ref[1] = jnp.zeros(acc_ref.shape[1:], F32)

    @pl.when(j == 0)
    def _():
        y = _layer_norm(acc_ref[1 - cur], g_ref[...], b_ref[...])
        o_ref[...] = y
        if emit_bf16:
            ob_ref[...] = y.astype(BF16)
        xb_ref[...] = x_ref[...].astype(BF16)
        gate_up()

    @pl.when((j == 1) & (i < n_rows))
    def _():
        acc_ref[cur] = ALPHA * x_ref[...] + down()
        gate_up()

    @pl.when((j > 1) & (j < n_chunks) & (i < n_rows))
    def _():
        acc_ref[cur] += down()
        gate_up()

    @pl.when((j == n_chunks) & (i < n_rows))
    def _():
        acc_ref[cur] += down()


def _ffn_ln(x, wg, wu, wd, g, b, emit_bf16):
    n, d = x.shape
    f = wg.shape[1]
    tm = min(FFN_ROWS, n)
    tf = FFN_COLS
    n_rows, n_chunks = n // tm, f // tf
    last = n_chunks - 1

    def up_chunk(i, j):
        return jnp.where(i == n_rows, last, jnp.minimum(j, last))

    def down_chunk(i, j):
        return jnp.where(i == n_rows, last, jnp.maximum(j - 1, 0))

    out_rows = pl.BlockSpec((tm, d), lambda i, j: (jnp.maximum(i - 1, 0), 0))
    return pl.pallas_call(
        functools.partial(_ffn_ln_kernel, emit_bf16=emit_bf16),
        grid=(n_rows + 1, n_chunks + 1),
        in_specs=[
            pl.BlockSpec((tm, d), lambda i, j: (jnp.minimum(i, n_rows - 1), 0)),
            pl.BlockSpec((d, tf), lambda i, j: (0, up_chunk(i, j))),
            pl.BlockSpec((d, tf), lambda i, j: (0, up_chunk(i, j))),
            pl.BlockSpec((tf, d), lambda i, j: (down_chunk(i, j), 0)),
            pl.BlockSpec((1, d), lambda i, j: (0, 0)),
            pl.BlockSpec((1, d), lambda i, j: (0, 0)),
        ],
        out_specs=[out_rows, out_rows] if emit_bf16 else out_rows,
        out_shape=([jax.ShapeDtypeStruct((n, d), F32), jax.ShapeDtypeStruct((n, d), BF16)] if emit_bf16
                   else jax.ShapeDtypeStruct((n, d), F32)),
        scratch_shapes=[pltpu.VMEM((tm, d), BF16), pltpu.VMEM((2, tm, d), F32),
                        pltpu.VMEM((2, tm, tf), BF16)],
        compiler_params=_params("arbitrary", "arbitrary"),
        name="ffn_ln",
    )(x, wg, wu, wd, g, b)


def _rotary(h, cos, sin_lo, sin_hi):
    half = ROT_DIM // 2
    return (h * cos + pltpu.roll(h, half, 1) * sin_hi
            + pltpu.roll(h, LANES - half, 1) * sin_lo)


def _inproj_kernel(xb_ref, w_ref, cos_ref, slo_ref, shi_ref, u_ref, qt_ref, k_ref, vt_ref, g_ref):
    j = pl.program_id(1)

    def for_chunks(store):
        xb = xb_ref[...]
        for c in range(w_ref.shape[1] // PROJ_CHUNK):
            cols = slice(c * PROJ_CHUNK, (c + 1) * PROJ_CHUNK)
            h = jnp.dot(xb, w_ref[:, cols], preferred_element_type=F32)
            for s in range(PROJ_CHUNK // LANES):
                lo = c * PROJ_CHUNK + s * LANES
                store(slice(lo, lo + LANES), h[:, s * LANES:(s + 1) * LANES])

    def rope(h):
        return _rotary(h, cos_ref[...], slo_ref[...], shi_ref[...])

    @pl.when(j == 0)
    def _():
        def store(cols, h):
            u_ref[:, cols] = h
        for_chunks(store)

    @pl.when(j == 1)
    def _():
        def store(cols, h):
            qt_ref[cols, :] = (rope(h) * Q_SCALE).T.astype(BF16)
        for_chunks(store)

    @pl.when(j == 2)
    def _():
        def store(cols, h):
            k_ref[:, cols] = rope(h).astype(BF16)
        for_chunks(store)

    @pl.when(j == 3)
    def _():
        def store(cols, h):
            base = cols.start // LANES * VT_ROWS
            vt_ref[base:base + LANES, :] = h.T.astype(BF16)
            vt_ref[base + LANES:base + VT_ROWS, :] = jnp.ones((VT_ROWS - LANES, h.shape[0]), BF16)
        for_chunks(store)

    @pl.when(j >= 4)
    def _():
        def store(cols, h):
            g_ref[:, cols] = _sigmoid(h).astype(BF16)
        for_chunks(store)


def _inproj(x, w, cos, slo, shi):
    n, d = x.shape
    tm = min(PROJ_ROWS, n)
    tn = PROJ_COLS
    width = w.shape[1]
    n_gate = (width - SSM_WIDTH - 2 * QK_WIDTH - V_WIDTH) // tn
    assert SSM_WIDTH == tn and QK_WIDTH == tn and V_WIDTH == tn
    tab = pl.BlockSpec((tm, LANES), lambda i, j: (i, 0))
    return pl.pallas_call(
        _inproj_kernel,
        grid=(n // tm, width // tn),
        in_specs=[
            pl.BlockSpec((tm, d), lambda i, j: (i, 0)),
            pl.BlockSpec((d, tn), lambda i, j: (0, j)),
            tab, tab, tab,
        ],
        out_specs=[
            pl.BlockSpec((tm, tn), lambda i, j: (i, 0)),
            pl.BlockSpec((tn, tm), lambda i, j: (0, i)),
            pl.BlockSpec((tm, tn), lambda i, j: (i, 0)),
            pl.BlockSpec((N_HEADS * VT_ROWS, tm), lambda i, j: (0, i)),
            pl.BlockSpec((tm, tn), lambda i, j: (i, jnp.clip(j - 4, 0, n_gate - 1))),
        ],
        out_shape=[
            jax.ShapeDtypeStruct((n, SSM_WIDTH), F32),
            jax.ShapeDtypeStruct((QK_WIDTH, n), BF16),
            jax.ShapeDtypeStruct((n, QK_WIDTH), BF16),
            jax.ShapeDtypeStruct((N_HEADS * VT_ROWS, n), BF16),
            jax.ShapeDtypeStruct((n, n_gate * tn), BF16),
        ],
        compiler_params=_params("parallel", "arbitrary"),
        name="inproj",
    )(x, w, cos, slo, shi)


def _ssm_kernel(u_ref, bbd_ref, cbd_ref, pn_re_ref, pn_im_ref, pp_re_ref, pp_im_ref,
                lb_re_ref, lb_im_ref, d_ref, tri_ref, y_ref, car_re_ref, car_im_ref):
    @pl.when(pl.program_id(0) == 0)
    def _():
        car_re_ref[...] = jnp.zeros_like(car_re_ref)
        car_im_ref[...] = jnp.zeros_like(car_im_ref)

    t_len = pn_re_ref.shape[0]
    n_sub = u_ref.shape[0] // t_len
    slab_in = u_ref.shape[1] // SSM_SLABS
    slab_st = pn_re_ref.shape[1] // SSM_SLABS
    u = u_ref[...]
    ub = u.astype(BF16)
    tri = tri_ref[...]
    for s in range(SSM_SLABS):
        cin = slice(s * slab_in, (s + 1) * slab_in)
        cst = slice(s * slab_st, (s + 1) * slab_st)
        bu = jnp.dot(ub[:, cin], bbd_ref[s], preferred_element_type=F32)
        nr, ni = pn_re_ref[:, cst], pn_im_ref[:, cst]
        z = []
        for k in range(n_sub):
            br = bu[k * t_len:(k + 1) * t_len, :slab_st]
            bi = bu[k * t_len:(k + 1) * t_len, slab_st:]
            z.append(jnp.concatenate([br * nr - bi * ni, br * ni + bi * nr], axis=1).astype(BF16))
        cs = jnp.dot(tri, jnp.concatenate(z, axis=0), preferred_element_type=F32)
        lr, li = lb_re_ref[:, cst], lb_im_ref[:, cst]
        pr, pi = pp_re_ref[:, cst], pp_im_ref[:, cst]
        kr, ki = car_re_ref[:, cst], car_im_ref[:, cst]
        states = []
        for k in range(n_sub):
            cr = cs[k * t_len:(k + 1) * t_len, :slab_st] + (lr * kr - li * ki)
            ci = cs[k * t_len:(k + 1) * t_len, slab_st:] + (lr * ki + li * kr)
            sr = cr * pr - ci * pi
            si = cr * pi + ci * pr
            kr, ki = sr[t_len - 1:t_len, :], si[t_len - 1:t_len, :]
            states.append(jnp.concatenate([sr, si], axis=1).astype(BF16))
        car_re_ref[:, cst] = kr
        car_im_ref[:, cst] = ki
        y = jnp.dot(jnp.concatenate(states, axis=0), cbd_ref[s], preferred_element_type=F32)
        y_ref[:, cin] = y + d_ref[:, cin] * u[:, cin]


def _ssm(u, bbd, cbd, pn_re, pn_im, pp_re, pp_im, lb_re, lb_im, d_skip, tri):
    n, w = u.shape
    t = SSM_CHUNK
    rows = tri.shape[0]
    ns = pn_re.shape[1]
    const2 = lambda shape: pl.BlockSpec(shape, lambda i: (0, 0))
    const3 = lambda shape: pl.BlockSpec(shape, lambda i: (0, 0, 0))
    return pl.pallas_call(
        _ssm_kernel,
        grid=(n // rows,),
        in_specs=[
            pl.BlockSpec((rows, w), lambda i: (i, 0)),
            const3(bbd.shape), const3(cbd.shape),
            const2((t, ns)), const2((t, ns)), const2((t, ns)), const2((t, ns)),
            const2((1, ns)), const2((1, ns)), const2((1, w)), const2((rows, rows)),
        ],
        out_specs=pl.BlockSpec((rows, w), lambda i: (i, 0)),
        out_shape=jax.ShapeDtypeStruct((n, w), F32),
        scratch_shapes=[pltpu.VMEM((1, ns), F32), pltpu.VMEM((1, ns), F32)],
        compiler_params=_params("arbitrary"),
        name="ssm_scan",
    )(u, bbd, cbd, pn_re, pn_im, pp_re, pp_im, lb_re, lb_im, d_skip, tri)


def _complex_powers(lr, li, count):
    pr, pi = jnp.ones_like(lr), jnp.zeros_like(li)
    cr, ci = lr, li
    rows = 1
    while rows < count:
        pr, pi = (jnp.concatenate([pr, pr * cr - pi * ci], axis=0),
                  jnp.concatenate([pi, pr * ci + pi * cr], axis=0))
        cr, ci = cr * cr - ci * ci, 2.0 * cr * ci
        rows *= 2
    return pr[:count], pi[:count]


def _ssm_tables(a_re, a_im, b_re, b_im, c_re, c_im, log_dt):
    g, p = a_re.shape
    h = b_re.shape[2]
    dt = jnp.exp(log_dt)[:, None]
    mag = jnp.exp(a_re * dt)
    lb_re = mag * jnp.cos(a_im * dt)
    lb_im = mag * jnp.sin(a_im * dt)
    nr, ni = lb_re - 1.0, lb_im
    den = a_re * a_re + a_im * a_im
    coef_re = (nr * a_re + ni * a_im) / den
    coef_im = (ni * a_re - nr * a_im) / den
    bb_re = coef_re[:, :, None] * b_re - coef_im[:, :, None] * b_im
    bb_im = coef_re[:, :, None] * b_im + coef_im[:, :, None] * b_re
    gs = g // SSM_SLABS
    eye = jnp.eye(gs, dtype=F32)

    def in_blocks(bb):
        t = bb.reshape(SSM_SLABS, gs, p, h).transpose(0, 1, 3, 2)
        return jnp.einsum("ab,sbhp->sahbp", eye, t).reshape(SSM_SLABS, gs * h, gs * p)

    def out_blocks(cc):
        t = cc.reshape(SSM_SLABS, gs, h, p).transpose(0, 1, 3, 2)
        return jnp.einsum("ab,sbph->sbpah", eye, t).reshape(SSM_SLABS, gs * p, gs * h)

    bbd = jnp.concatenate([in_blocks(bb_re), in_blocks(bb_im)], axis=2).astype(BF16)
    cbd = jnp.concatenate([out_blocks(c_re), -out_blocks(c_im)], axis=1).astype(BF16)
    lr, li = lb_re.reshape(1, g * p), lb_im.reshape(1, g * p)
    inv_den = lr * lr + li * li
    pp_re, pp_im = _complex_powers(lr, li, SSM_CHUNK)
    pn_re, pn_im = _complex_powers(lr / inv_den, -li / inv_den, SSM_CHUNK)
    return bbd, cbd, pn_re, pn_im, pp_re, pp_im, lr, li


def _attn_kernel(lam_ref, qt_ref, k_ref, vt_ref, g_ref, o_ref,
                 qz_ref, sa_ref, sb_ref, mxa_ref, mxb_ref, m_ref, acc_ref, *, tk, out_scale):
    i = pl.program_id(1)
    tq = qt_ref.shape[1]
    assert tq == tk
    qt = qt_ref[...]
    comp = lax.broadcasted_iota(jnp.int32, qt.shape, 0) < HEAD_DIM
    zero = jnp.zeros_like(qt)
    qz_ref[0] = jnp.where(comp, qt, zero)
    qz_ref[1] = jnp.where(comp, zero, qt)
    m_ref[...] = jnp.full_like(m_ref, NEG_INF)
    acc_ref[...] = jnp.zeros_like(acc_ref)

    def scores(j, s_ref, mx_ref, masked):
        start = pl.multiple_of(j * tk, tk)
        kb = k_ref[pl.ds(start, tk), :]
        for c in range(2):
            s = jnp.dot(kb, qz_ref[c], preferred_element_type=F32)
            if masked:
                key = lax.broadcasted_iota(jnp.int32, s.shape, 0)
                qry = lax.broadcasted_iota(jnp.int32, s.shape, 1)
                s = jnp.where(key <= qry, s, NEG_INF)
            s_ref[c] = s
            mx_ref[c] = jnp.max(s, axis=0, keepdims=True)

    def update(j, s_ref, mx_ref):
        start = pl.multiple_of(j * tk, tk)
        vtb = vt_ref[:, pl.ds(start, tk)]
        for c in range(2):
            m_prev = m_ref[c]
            m_new = jnp.maximum(m_prev, mx_ref[c])
            alpha = jnp.exp2(m_prev - m_new)
            p = jnp.exp2(s_ref[c] - m_new).astype(BF16)
            acc_ref[c] = alpha * acc_ref[c] + jnp.dot(vtb, p, preferred_element_type=F32)
            m_ref[c] = m_new

    @pl.when(i == 0)
    def _():
        scores(0, sa_ref, mxa_ref, True)
        update(0, sa_ref, mxa_ref)

    @pl.when(i > 0)
    def _():
        scores(0, sa_ref, mxa_ref, False)

    def pair(jj, carry):
        a = 2 * jj
        scores(a + 1, sb_ref, mxb_ref, False)
        update(a, sa_ref, mxa_ref)
        scores(a + 2, sa_ref, mxa_ref, False)
        update(a + 1, sb_ref, mxb_ref)
        return carry

    n_pairs = lax.shift_right_logical(jnp.maximum(i - 1, 0), 1)
    lax.fori_loop(0, n_pairs, pair, 0)
    first = 2 * n_pairs

    @pl.when(i % 2 == 1)
    def _():
        scores(i, sb_ref, mxb_ref, True)
        update(first, sa_ref, mxa_ref)
        update(i, sb_ref, mxb_ref)

    @pl.when((i % 2 == 0) & (i > 0))
    def _():
        scores(first + 1, sb_ref, mxb_ref, False)
        update(first, sa_ref, mxa_ref)
        scores(i, sa_ref, mxa_ref, True)
        update(first + 1, sb_ref, mxb_ref)
        update(i, sa_ref, mxa_ref)

    hw = 2 * HEAD_DIM
    o1 = acc_ref[0, :hw, :] / acc_ref[0, hw:hw + 1, :]
    o2 = acc_ref[1, :hw, :] / acc_ref[1, hw:hw + 1, :]
    ot = o1 - lam_ref[0, 0] * o2
    ms = jnp.mean(ot * ot, axis=0, keepdims=True)
    o = (ot * lax.rsqrt(ms + RMS_EPS)).T
    o_ref[...] = (o * g_ref[...] * out_scale).astype(o_ref.dtype)


def _attention(qt, k, vt, lam, subln_g, lam_init):
    n = k.shape[0]
    tq = min(ATTN_Q, n)
    tk = min(ATTN_K, n)
    hw = 2 * HEAD_DIM
    kernel = functools.partial(_attn_kernel, tk=tk, out_scale=1.0 - lam_init)
    return pl.pallas_call(
        kernel,
        grid=(N_HEADS, n // tq),
        in_specs=[
            pl.BlockSpec(memory_space=pltpu.SMEM),
            pl.BlockSpec((hw, tq), lambda h, i: (h, i)),
            pl.BlockSpec((n, hw), lambda h, i: (0, h)),
            pl.BlockSpec((VT_ROWS, n), lambda h, i: (h, 0)),
            pl.BlockSpec((1, hw), lambda h, i: (0, 0)),
        ],
        out_specs=pl.BlockSpec((tq, hw), lambda h, i: (i, h)),
        out_shape=jax.ShapeDtypeStruct((n, V_WIDTH), BF16),
        scratch_shapes=[
            pltpu.VMEM((2, hw, tq), BF16),
            pltpu.VMEM((2, tk, tq), F32), pltpu.VMEM((2, tk, tq), F32),
            pltpu.VMEM((2, 1, tq), F32), pltpu.VMEM((2, 1, tq), F32),
            pltpu.VMEM((2, 1, tq), F32),
            pltpu.VMEM((2, VT_ROWS, tq), F32),
        ],
        compiler_params=_params("parallel", "arbitrary"),
        name="diff_attn",
    )(lam, qt, k, vt, subln_g)


def _mix_kernel(x_ref, ys_ref, at_ref, sg_ref, wglu_ref, wbs_ref, wba_ref, wout_ref,
                g_ref, b_ref, o_ref):
    d = x_ref.shape[1]
    y = ys_ref[...]
    z = 0.5 * y * (1.0 + lax.erf(y * (2.0 ** -0.5)))
    gate = _sigmoid(jnp.dot(z.astype(BF16), wglu_ref[...], preferred_element_type=F32))
    glu = (z * gate).astype(BF16)
    ya = jnp.dot(glu, wbs_ref[...], preferred_element_type=F32)
    yb = jnp.dot(at_ref[...], wba_ref[...], preferred_element_type=F32)
    sg = sg_ref[...].astype(F32)
    mix = (sg[:, :d] * ya + sg[:, d:] * yb).astype(BF16)
    out = ALPHA * x_ref[...] + jnp.dot(mix, wout_ref[...], preferred_element_type=F32)
    o_ref[...] = _layer_norm(out, g_ref[...], b_ref[...])


def _resident(shape):
    return pl.BlockSpec(shape, lambda i: (0,) * len(shape), pipeline_mode=pl.Buffered(1))


def _mix(x, ys, attn, sg, wglu, wbs, wba, wout, g, b):
    n, d = x.shape
    tm = min(MIX_ROWS, n)
    rows = lambda w: pl.BlockSpec((tm, w), lambda i: (i, 0))
    return pl.pallas_call(
        _mix_kernel,
        grid=(n // tm,),
        in_specs=[
            rows(d), rows(ys.shape[1]), rows(attn.shape[1]), rows(sg.shape[1]),
            _resident(wglu.shape), _resident(wbs.shape), _resident(wba.shape), _resident(wout.shape),
            _resident((1, d)), _resident((1, d)),
        ],
        out_specs=rows(d),
        out_shape=jax.ShapeDtypeStruct((n, d), F32),
        compiler_params=_params("parallel"),
        name="mixer_out",
    )(x, ys, attn, sg, wglu, wbs, wba, wout, g, b)


def _ple_kernel(x_ref, p_ref, wg_ref, wp_ref, g_ref, b_ref, o_ref):
    x = x_ref[...]
    gate = _sigmoid(jnp.dot(x.astype(BF16), wg_ref[...], preferred_element_type=F32))
    emb = jnp.dot(p_ref[...].astype(BF16), wp_ref[...], preferred_element_type=F32)
    o_ref[...] = _layer_norm(ALPHA * x + gate * emb, g_ref[...], b_ref[...])


def _ple(x, p, wg, wp, g, b):
    n, d = x.shape
    tm = min(PLE_ROWS, n)
    rows = lambda w: pl.BlockSpec((tm, w), lambda i: (i, 0))
    return pl.pallas_call(
        _ple_kernel,
        grid=(n // tm,),
        in_specs=[rows(d), rows(p.shape[1]), _resident(wg.shape), _resident(wp.shape),
                  _resident((1, d)), _resident((1, d))],
        out_specs=rows(d),
        out_shape=jax.ShapeDtypeStruct((n, d), F32),
        compiler_params=_params("parallel"),
        name="ple_ln",
    )(x, p, wg, wp, g, b)


def _rope_tables(positions):
    half = ROT_DIM // 2
    inv_freq = ROPE_THETA ** (-jnp.arange(0, ROT_DIM, 2, dtype=F32) / ROT_DIM)
    ang = positions.astype(F32)[:, None] * inv_freq
    cos, sin = jnp.cos(ang), jnp.sin(ang)
    n = positions.shape[0]
    pad = jnp.zeros((n, HEAD_DIM - ROT_DIM), F32)
    zeros = jnp.zeros((n, half), F32)
    cos_t = jnp.concatenate([cos, cos, pad + 1.0], axis=1)
    sin_lo = jnp.concatenate([-sin, zeros, pad], axis=1)
    sin_hi = jnp.concatenate([zeros, sin, pad], axis=1)
    reps = LANES // HEAD_DIM
    return tuple(jnp.tile(t, (1, reps)) for t in (cos_t, sin_lo, sin_hi))


def kernel(x, p, positions, ffn1_w_gate, ffn1_w_up, ffn1_w_down, ln1_g, ln1_b, w_in, ssm_a_re, ssm_a_im, ssm_b_re, ssm_b_im, ssm_c_re, ssm_c_im, ssm_log_dt, ssm_d, ssm_w_glu, w_branch_ssm, lambda_q1, lambda_k1, lambda_q2, lambda_k2, attn_subln_g, w_branch_attn, w_out, ln2_g, ln2_b, ffn2_w_gate, ffn2_w_up, ffn2_w_down, ln3_g, ln3_b, ple_w_gate, ple_w_proj, ln4_g, ln4_b):
    bsz, length, d = x.shape
    assert bsz == 1
    xs = x.reshape(length, d)
    cos, slo, shi = _rope_tables(positions[0])
    tri = jnp.kron(jnp.eye(SSM_ROWS // SSM_CHUNK, dtype=F32),
                   jnp.tril(jnp.ones((SSM_CHUNK, SSM_CHUNK), F32))).astype(BF16)
    row = lambda a: a.reshape(1, -1)
    for i in range(DEPTH):
        lam_init = 0.8 - 0.6 * math.exp(-0.3 * i)
        lam = (jnp.exp(jnp.sum(lambda_q1[i] * lambda_k1[i])) - jnp.exp(jnp.sum(lambda_q2[i] * lambda_k2[i]))
               + lam_init).reshape(1, 1)
        xs, xb = _ffn_ln(xs, ffn1_w_gate[i].astype(BF16), ffn1_w_up[i].astype(BF16),
                         (0.5 * ffn1_w_down[i]).astype(BF16), row(ln1_g[i]), row(ln1_b[i]), emit_bf16=True)
        u, qt, k, vt, sg = _inproj(xb, w_in[i].astype(BF16), cos, slo, shi)
        tables = _ssm_tables(ssm_a_re[i], ssm_a_im[i], ssm_b_re[i], ssm_b_im[i],
                             ssm_c_re[i], ssm_c_im[i], ssm_log_dt[i])
        ys = _ssm(u, *tables, row(ssm_d[i]), tri)
        attn = _attention(qt, k, vt, lam, row(attn_subln_g[i]), lam_init)
        xs = _mix(xs, ys, attn, sg, ssm_w_glu[i].astype(BF16), w_branch_ssm[i].astype(BF16),
                  w_branch_attn[i].astype(BF16), w_out[i].astype(BF16), row(ln2_g[i]), row(ln2_b[i]))
        xs = _ffn_ln(xs, ffn2_w_gate[i].astype(BF16), ffn2_w_up[i].astype(BF16),
                     (0.5 * ffn2_w_down[i]).astype(BF16), row(ln3_g[i]), row(ln3_b[i]), emit_bf16=False)
        xs = _ple(xs, p[i, 0], ple_w_gate[i].astype(BF16), ple_w_proj[i].astype(BF16),
                  row(ln4_g[i]), row(ln4_b[i]))
    return xs.reshape(bsz, length, d)
```

```python
import functools
import math

import jax
import jax.numpy as jnp
from jax import lax
from jax.experimental import pallas as pl
from jax.experimental.pallas import tpu as pltpu

F32 = jnp.float32
BF16 = jnp.bfloat16

DEPTH = 2
SSM_WIDTH = 1024
SSM_GROUP = 16
SSM_GROUPS = SSM_WIDTH // SSM_GROUP
SSM_STATE = 64
N_HEADS = 8
HEAD_DIM = 64
QK_WIDTH = N_HEADS * 2 * HEAD_DIM
V_WIDTH = N_HEADS * 2 * HEAD_DIM
ROT_DIM = HEAD_DIM // 4
ROPE_THETA = 500000.0
LN_EPS = 1e-5
RMS_EPS = 1e-5
NEG_INF = -1e30
ALPHA = (2 * DEPTH) ** 0.25
Q_SCALE = HEAD_DIM ** -0.5 * math.log2(math.e)

LANES = 128
BF16_ROWS = 16
VT_ROWS = 2 * HEAD_DIM + BF16_ROWS
VMEM_LIMIT_BYTES = 56 * 1024 * 1024

FFN_ROWS = 512
FFN_COLS = 512
PROJ_ROWS = 1024
PROJ_COLS = 1024
PROJ_CHUNK = 256
SSM_CHUNK = 128
SSM_ROWS = 256
SSM_SLABS = 4
ATTN_Q = 1024
ATTN_K = 1024
MIX_ROWS = 256
PLE_ROWS = 512


def _sigmoid(x):
    return 1.0 / (1.0 + jnp.exp(-x))


def _layer_norm(y, g, b):
    mu = jnp.mean(y, axis=-1, keepdims=True)
    yc = y - mu
    var = jnp.mean(yc * yc, axis=-1, keepdims=True)
    return yc * lax.rsqrt(var + LN_EPS) * g + b


def _params(*semantics):
    return pltpu.CompilerParams(dimension_semantics=semantics, vmem_limit_bytes=VMEM_LIMIT_BYTES)


def _ffn_ln_kernel(x_ref, wg_ref, wu_ref, wd_ref, g_ref, b_ref, *rest, emit_bf16):
    if emit_bf16:
        o_ref, ob_ref, xb_ref, acc_ref = rest
    else:
        o_ref, xb_ref, acc_ref = rest
    j = pl.program_id(1)

    @pl.when(j == 0)
    def _():
        xb_ref[...] = x_ref[...].astype(BF16)
        acc_ref[...] = jnp.zeros_like(acc_ref)

    xb = xb_ref[...]
    hg = jnp.dot(xb, wg_ref[...], preferred_element_type=F32)
    hu = jnp.dot(xb, wu_ref[...], preferred_element_type=F32)
    act = (hg * _sigmoid(hg) * hu).astype(BF16)
    acc_ref[...] += jnp.dot(act, wd_ref[...], preferred_element_type=F32)

    @pl.when(j == pl.num_programs(1) - 1)
    def _():
        y = _layer_norm(ALPHA * x_ref[...] + acc_ref[...], g_ref[...], b_ref[...])
        o_ref[...] = y
        if emit_bf16:
            ob_ref[...] = y.astype(BF16)


def _ffn_ln(x, wg, wu, wd, g, b, emit_bf16):
    n, d = x.shape
    f = wg.shape[1]
    tm = min(FFN_ROWS, n)
    tf = FFN_COLS
    rows = pl.BlockSpec((tm, d), lambda i, j: (i, 0))
    return pl.pallas_call(
        functools.partial(_ffn_ln_kernel, emit_bf16=emit_bf16),
        grid=(n // tm, f // tf),
        in_specs=[
            rows,
            pl.BlockSpec((d, tf), lambda i, j: (0, j)),
            pl.BlockSpec((d, tf), lambda i, j: (0, j)),
            pl.BlockSpec((tf, d), lambda i, j: (j, 0)),
            pl.BlockSpec((1, d), lambda i, j: (0, 0)),
            pl.BlockSpec((1, d), lambda i, j: (0, 0)),
        ],
        out_specs=[rows, rows] if emit_bf16 else rows,
        out_shape=([jax.ShapeDtypeStruct((n, d), F32), jax.ShapeDtypeStruct((n, d), BF16)] if emit_bf16
                   else jax.ShapeDtypeStruct((n, d), F32)),
        scratch_shapes=[pltpu.VMEM((tm, d), BF16), pltpu.VMEM((tm, d), F32)],
        compiler_params=_params("parallel", "arbitrary"),
        name="ffn_ln",
    )(x, wg, wu, wd, g, b)


def _rotary(h, cos, sin_lo, sin_hi):
    half = ROT_DIM // 2
    return (h * cos + pltpu.roll(h, half, 1) * sin_hi
            + pltpu.roll(h, LANES - half, 1) * sin_lo)


def _inproj_kernel(xb_ref, w_ref, cos_ref, slo_ref, shi_ref, u_ref, qt_ref, k_ref, vt_ref, g_ref):
    j = pl.program_id(1)

    def for_chunks(store):
        xb = xb_ref[...]
        for c in range(w_ref.shape[1] // PROJ_CHUNK):
            cols = slice(c * PROJ_CHUNK, (c + 1) * PROJ_CHUNK)
            h = jnp.dot(xb, w_ref[:, cols], preferred_element_type=F32)
            for s in range(PROJ_CHUNK // LANES):
                lo = c * PROJ_CHUNK + s * LANES
                store(slice(lo, lo + LANES), h[:, s * LANES:(s + 1) * LANES])

    def rope(h):
        return _rotary(h, cos_ref[...], slo_ref[...], shi_ref[...])

    @pl.when(j == 0)
    def _():
        def store(cols, h):
            u_ref[:, cols] = h
        for_chunks(store)

    @pl.when(j == 1)
    def _():
        def store(cols, h):
            qt_ref[cols, :] = (rope(h) * Q_SCALE).T.astype(BF16)
        for_chunks(store)

    @pl.when(j == 2)
    def _():
        def store(cols, h):
            k_ref[:, cols] = rope(h).astype(BF16)
        for_chunks(store)

    @pl.when(j == 3)
    def _():
        def store(cols, h):
            base = cols.start // LANES * VT_ROWS
            vt_ref[base:base + LANES, :] = h.T.astype(BF16)
            vt_ref[base + LANES:base + VT_ROWS, :] = jnp.ones((VT_ROWS - LANES, h.shape[0]), BF16)
        for_chunks(store)

    @pl.when(j >= 4)
    def _():
        def store(cols, h):
            g_ref[:, cols] = _sigmoid(h).astype(BF16)
        for_chunks(store)


def _inproj(x, w, cos, slo, shi):
    n, d = x.shape
    tm = min(PROJ_ROWS, n)
    tn = PROJ_COLS
    width = w.shape[1]
    n_gate = (width - SSM_WIDTH - 2 * QK_WIDTH - V_WIDTH) // tn
    assert SSM_WIDTH == tn and QK_WIDTH == tn and V_WIDTH == tn
    tab = pl.BlockSpec((tm, LANES), lambda i, j: (i, 0))
    return pl.pallas_call(
        _inproj_kernel,
        grid=(n // tm, width // tn),
        in_specs=[
            pl.BlockSpec((tm, d), lambda i, j: (i, 0)),
            pl.BlockSpec((d, tn), lambda i, j: (0, j)),
            tab, tab, tab,
        ],
        out_specs=[
            pl.BlockSpec((tm, tn), lambda i, j: (i, 0)),
            pl.BlockSpec((tn, tm), lambda i, j: (0, i)),
            pl.BlockSpec((tm, tn), lambda i, j: (i, 0)),
            pl.BlockSpec((N_HEADS * VT_ROWS, tm), lambda i, j: (0, i)),
            pl.BlockSpec((tm, tn), lambda i, j: (i, jnp.clip(j - 4, 0, n_gate - 1))),
        ],
        out_shape=[
            jax.ShapeDtypeStruct((n, SSM_WIDTH), F32),
            jax.ShapeDtypeStruct((QK_WIDTH, n), BF16),
            jax.ShapeDtypeStruct((n, QK_WIDTH), BF16),
            jax.ShapeDtypeStruct((N_HEADS * VT_ROWS, n), BF16),
            jax.ShapeDtypeStruct((n, n_gate * tn), BF16),
        ],
        compiler_params=_params("parallel", "arbitrary"),
        name="inproj",
    )(x, w, cos, slo, shi)


def _ssm_kernel(u_ref, bbd_ref, cbd_ref, pn_re_ref, pn_im_ref, pp_re_ref, pp_im_ref,
                lb_re_ref, lb_im_ref, d_ref, tri_ref, y_ref, car_re_ref, car_im_ref):
    @pl.when(pl.program_id(0) == 0)
    def _():
        car_re_ref[...] = jnp.zeros_like(car_re_ref)
        car_im_ref[...] = jnp.zeros_like(car_im_ref)

    t_len = pn_re_ref.shape[0]
    n_sub = u_ref.shape[0] // t_len
    slab_in = u_ref.shape[1] // SSM_SLABS
    slab_st = pn_re_ref.shape[1] // SSM_SLABS
    u = u_ref[...]
    ub = u.astype(BF16)
    tri = tri_ref[...]
    for s in range(SSM_SLABS):
        cin = slice(s * slab_in, (s + 1) * slab_in)
        cst = slice(s * slab_st, (s + 1) * slab_st)
        bu = jnp.dot(ub[:, cin], bbd_ref[s], preferred_element_type=F32)
        nr, ni = pn_re_ref[:, cst], pn_im_ref[:, cst]
        z = []
        for k in range(n_sub):
            br = bu[k * t_len:(k + 1) * t_len, :slab_st]
            bi = bu[k * t_len:(k + 1) * t_len, slab_st:]
            z.append(jnp.concatenate([br * nr - bi * ni, br * ni + bi * nr], axis=1).astype(BF16))
        cs = jnp.dot(tri, jnp.concatenate(z, axis=0), preferred_element_type=F32)
        lr, li = lb_re_ref[:, cst], lb_im_ref[:, cst]
        pr, pi = pp_re_ref[:, cst], pp_im_ref[:, cst]
        kr, ki = car_re_ref[:, cst], car_im_ref[:, cst]
        states = []
        for k in range(n_sub):
            cr = cs[k * t_len:(k + 1) * t_len, :slab_st] + (lr * kr - li * ki)
            ci = cs[k * t_len:(k + 1) * t_len, slab_st:] + (lr * ki + li * kr)
            sr = cr * pr - ci * pi
            si = cr * pi + ci * pr
            kr, ki = sr[t_len - 1:t_len, :], si[t_len - 1:t_len, :]
            states.append(jnp.concatenate([sr, si], axis=1).astype(BF16))
        car_re_ref[:, cst] = kr
        car_im_ref[:, cst] = ki
        y = jnp.dot(jnp.concatenate(states, axis=0), cbd_ref[s], preferred_element_type=F32)
        y_ref[:, cin] = y + d_ref[:, cin] * u[:, cin]


def _ssm(u, bbd, cbd, pn_re, pn_im, pp_re, pp_im, lb_re, lb_im, d_skip, tri):
    n, w = u.shape
    t = SSM_CHUNK
    rows = tri.shape[0]
    ns = pn_re.shape[1]
    const2 = lambda shape: pl.BlockSpec(shape, lambda i: (0, 0))
    const3 = lambda shape: pl.BlockSpec(shape, lambda i: (0, 0, 0))
    return pl.pallas_call(
        _ssm_kernel,
        grid=(n // rows,),
        in_specs=[
            pl.BlockSpec((rows, w), lambda i: (i, 0)),
            const3(bbd.shape), const3(cbd.shape),
            const2((t, ns)), const2((t, ns)), const2((t, ns)), const2((t, ns)),
            const2((1, ns)), const2((1, ns)), const2((1, w)), const2((rows, rows)),
        ],
        out_specs=pl.BlockSpec((rows, w), lambda i: (i, 0)),
        out_shape=jax.ShapeDtypeStruct((n, w), F32),
        scratch_shapes=[pltpu.VMEM((1, ns), F32), pltpu.VMEM((1, ns), F32)],
        compiler_params=_params("arbitrary"),
        name="ssm_scan",
    )(u, bbd, cbd, pn_re, pn_im, pp_re, pp_im, lb_re, lb_im, d_skip, tri)


def _complex_powers(lr, li, count):
    pr, pi = jnp.ones_like(lr), jnp.zeros_like(li)
    cr, ci = lr, li
    rows = 1
    while rows < count:
        pr, pi = (jnp.concatenate([pr, pr * cr - pi * ci], axis=0),
                  jnp.concatenate([pi, pr * ci + pi * cr], axis=0))
        cr, ci = cr * cr - ci * ci, 2.0 * cr * ci
        rows *= 2
    return pr[:count], pi[:count]


def _ssm_tables(a_re, a_im, b_re, b_im, c_re, c_im, log_dt):
    g, p = a_re.shape
    h = b_re.shape[2]
    dt = jnp.exp(log_dt)[:, None]
    mag = jnp.exp(a_re * dt)
    lb_re = mag * jnp.cos(a_im * dt)
    lb_im = mag * jnp.sin(a_im * dt)
    nr, ni = lb_re - 1.0, lb_im
    den = a_re * a_re + a_im * a_im
    coef_re = (nr * a_re + ni * a_im) / den
    coef_im = (ni * a_re - nr * a_im) / den
    bb_re = coef_re[:, :, None] * b_re - coef_im[:, :, None] * b_im
    bb_im = coef_re[:, :, None] * b_im + coef_im[:, :, None] * b_re
    gs = g // SSM_SLABS
    eye = jnp.eye(gs, dtype=F32)

    def in_blocks(bb):
        t = bb.reshape(SSM_SLABS, gs, p, h).transpose(0, 1, 3, 2)
        return jnp.einsum("ab,sbhp->sahbp", eye, t).reshape(SSM_SLABS, gs * h, gs * p)

    def out_blocks(cc):
        t = cc.reshape(SSM_SLABS, gs, h, p).transpose(0, 1, 3, 2)
        return jnp.einsum("ab,sbph->sbpah", eye, t).reshape(SSM_SLABS, gs * p, gs * h)

    bbd = jnp.concatenate([in_blocks(bb_re), in_blocks(bb_im)], axis=2).astype(BF16)
    cbd = jnp.concatenate([out_blocks(c_re), -out_blocks(c_im)], axis=1).astype(BF16)
    lr, li = lb_re.reshape(1, g * p), lb_im.reshape(1, g * p)
    inv_den = lr * lr + li * li
    pp_re, pp_im = _complex_powers(lr, li, SSM_CHUNK)
    pn_re, pn_im = _complex_powers(lr / inv_den, -li / inv_den, SSM_CHUNK)
    return bbd, cbd, pn_re, pn_im, pp_re, pp_im, lr, li


def _attn_kernel(lam_ref, qt_ref, k_ref, vt_ref, g_ref, o_ref,
                 qz_ref, sa_ref, sb_ref, mxa_ref, mxb_ref, m_ref, acc_ref, *, tk, out_scale):
    i = pl.program_id(1)
    tq = qt_ref.shape[1]
    assert tq == tk
    qt = qt_ref[...]
    comp = lax.broadcasted_iota(jnp.int32, qt.shape, 0) < HEAD_DIM
    zero = jnp.zeros_like(qt)
    qz_ref[0] = jnp.where(comp, qt, zero)
    qz_ref[1] = jnp.where(comp, zero, qt)
    m_ref[...] = jnp.full_like(m_ref, NEG_INF)
    acc_ref[...] = jnp.zeros_like(acc_ref)

    def scores(j, s_ref, mx_ref, masked):
        start = pl.multiple_of(j * tk, tk)
        kb = k_ref[pl.ds(start, tk), :]
        for c in range(2):
            s = jnp.dot(kb, qz_ref[c], preferred_element_type=F32)
            if masked:
                key = lax.broadcasted_iota(jnp.int32, s.shape, 0)
                qry = lax.broadcasted_iota(jnp.int32, s.shape, 1)
                s = jnp.where(key <= qry, s, NEG_INF)
            s_ref[c] = s
            mx_ref[c] = jnp.max(s, axis=0, keepdims=True)

    def update(j, s_ref, mx_ref):
        start = pl.multiple_of(j * tk, tk)
        vtb = vt_ref[:, pl.ds(start, tk)]
        for c in range(2):
            m_prev = m_ref[c]
            m_new = jnp.maximum(m_prev, mx_ref[c])
            alpha = jnp.exp2(m_prev - m_new)
            p = jnp.exp2(s_ref[c] - m_new).astype(BF16)
            acc_ref[c] = alpha * acc_ref[c] + jnp.dot(vtb, p, preferred_element_type=F32)
            m_ref[c] = m_new

    @pl.when(i == 0)
    def _():
        scores(0, sa_ref, mxa_ref, True)
        update(0, sa_ref, mxa_ref)

    @pl.when(i > 0)
    def _():
        scores(0, sa_ref, mxa_ref, False)

    def pair(jj, carry):
        a = 2 * jj
        scores(a + 1, sb_ref, mxb_ref, False)
        update(a, sa_ref, mxa_ref)
        scores(a + 2, sa_ref, mxa_ref, False)
        update(a + 1, sb_ref, mxb_ref)
        return carry

    n_pairs = lax.shift_right_logical(jnp.maximum(i - 1, 0), 1)
    lax.fori_loop(0, n_pairs, pair, 0)
    first = 2 * n_pairs

    @pl.when(i % 2 == 1)
    def _():
        scores(i, sb_ref, mxb_ref, True)
        update(first, sa_ref, mxa_ref)
        update(i, sb_ref, mxb_ref)

    @pl.when((i % 2 == 0) & (i > 0))
    def _():
        scores(first + 1, sb_ref, mxb_ref, False)
        update(first, sa_ref, mxa_ref)
        scores(i, sa_ref, mxa_ref, True)
        update(first + 1, sb_ref, mxb_ref)
        update(i, sa_ref, mxa_ref)

    hw = 2 * HEAD_DIM
    o1 = acc_ref[0, :hw, :] / acc_ref[0, hw:hw + 1, :]
    o2 = acc_ref[1, :hw, :] / acc_ref[1, hw:hw + 1, :]
    ot = o1 - lam_ref[0, 0] * o2
    ms = jnp.mean(ot * ot, axis=0, keepdims=True)
    o = (ot * lax.rsqrt(ms + RMS_EPS)).T
    o_ref[...] = (o * g_ref[...] * out_scale).astype(o_ref.dtype)


def _attention(qt, k, vt, lam, subln_g, lam_init):
    n = k.shape[0]
    tq = min(ATTN_Q, n)
    tk = min(ATTN_K, n)
    hw = 2 * HEAD_DIM
    kernel = functools.partial(_attn_kernel, tk=tk, out_scale=1.0 - lam_init)
    return pl.pallas_call(
        kernel,
        grid=(N_HEADS, n // tq),
        in_specs=[
            pl.BlockSpec(memory_space=pltpu.SMEM),
            pl.BlockSpec((hw, tq), lambda h, i: (h, i)),
            pl.BlockSpec((n, hw), lambda h, i: (0, h)),
            pl.BlockSpec((VT_ROWS, n), lambda h, i: (h, 0)),
            pl.BlockSpec((1, hw), lambda h, i: (0, 0)),
        ],
        out_specs=pl.BlockSpec((tq, hw), lambda h, i: (i, h)),
        out_shape=jax.ShapeDtypeStruct((n, V_WIDTH), BF16),
        scratch_shapes=[
            pltpu.VMEM((2, hw, tq), BF16),
            pltpu.VMEM((2, tk, tq), F32), pltpu.VMEM((2, tk, tq), F32),
            pltpu.VMEM((2, 1, tq), F32), pltpu.VMEM((2, 1, tq), F32),
            pltpu.VMEM((2, 1, tq), F32),
            pltpu.VMEM((2, VT_ROWS, tq), F32),
        ],
        compiler_params=_params("parallel", "arbitrary"),
        name="diff_attn",
    )(lam, qt, k, vt, subln_g)


def _mix_kernel(x_ref, ys_ref, at_ref, sg_ref, wglu_ref, wbs_ref, wba_ref, wout_ref,
                g_ref, b_ref, o_ref):
    d = x_ref.shape[1]
    y = ys_ref[...]
    z = 0.5 * y * (1.0 + lax.erf(y * (2.0 ** -0.5)))
    gate = _sigmoid(jnp.dot(z.astype(BF16), wglu_ref[...], preferred_element_type=F32))
    glu = (z * gate).astype(BF16)
    ya = jnp.dot(glu, wbs_ref[...], preferred_element_type=F32)
    yb = jnp.dot(at_ref[...], wba_ref[...], preferred_element_type=F32)
    sg = sg_ref[...].astype(F32)
    mix = (sg[:, :d] * ya + sg[:, d:] * yb).astype(BF16)
    out = ALPHA * x_ref[...] + jnp.dot(mix, wout_ref[...], preferred_element_type=F32)
    o_ref[...] = _layer_norm(out, g_ref[...], b_ref[...])


def _resident(shape):
    return pl.BlockSpec(shape, lambda i: (0,) * len(shape), pipeline_mode=pl.Buffered(1))


def _mix(x, ys, attn, sg, wglu, wbs, wba, wout, g, b):
    n, d = x.shape
    tm = min(MIX_ROWS, n)
    rows = lambda w: pl.BlockSpec((tm, w), lambda i: (i, 0))
    return pl.pallas_call(
        _mix_kernel,
        grid=(n // tm,),
        in_specs=[
            rows(d), rows(ys.shape[1]), rows(attn.shape[1]), rows(sg.shape[1]),
            _resident(wglu.shape), _resident(wbs.shape), _resident(wba.shape), _resident(wout.shape),
            _resident((1, d)), _resident((1, d)),
        ],
        out_specs=rows(d),
        out_shape=jax.ShapeDtypeStruct((n, d), F32),
        compiler_params=_params("parallel"),
        name="mixer_out",
    )(x, ys, attn, sg, wglu, wbs, wba, wout, g, b)


def _ple_kernel(x_ref, p_ref, wg_ref, wp_ref, g_ref, b_ref, o_ref):
    x = x_ref[...]
    gate = _sigmoid(jnp.dot(x.astype(BF16), wg_ref[...], preferred_element_type=F32))
    emb = jnp.dot(p_ref[...].astype(BF16), wp_ref[...], preferred_element_type=F32)
    o_ref[...] = _layer_norm(ALPHA * x + gate * emb, g_ref[...], b_ref[...])


def _ple(x, p, wg, wp, g, b):
    n, d = x.shape
    tm = min(PLE_ROWS, n)
    rows = lambda w: pl.BlockSpec((tm, w), lambda i: (i, 0))
    return pl.pallas_call(
        _ple_kernel,
        grid=(n // tm,),
        in_specs=[rows(d), rows(p.shape[1]), _resident(wg.shape), _resident(wp.shape),
                  _resident((1, d)), _resident((1, d))],
        out_specs=rows(d),
        out_shape=jax.ShapeDtypeStruct((n, d), F32),
        compiler_params=_params("parallel"),
        name="ple_ln",
    )(x, p, wg, wp, g, b)


def _rope_tables(positions):
    half = ROT_DIM // 2
    inv_freq = ROPE_THETA ** (-jnp.arange(0, ROT_DIM, 2, dtype=F32) / ROT_DIM)
    ang = positions.astype(F32)[:, None] * inv_freq
    cos, sin = jnp.cos(ang), jnp.sin(ang)
    n = positions.shape[0]
    pad = jnp.zeros((n, HEAD_DIM - ROT_DIM), F32)
    zeros = jnp.zeros((n, half), F32)
    cos_t = jnp.concatenate([cos, cos, pad + 1.0], axis=1)
    sin_lo = jnp.concatenate([-sin, zeros, pad], axis=1)
    sin_hi = jnp.concatenate([zeros, sin, pad], axis=1)
    reps = LANES // HEAD_DIM
    return tuple(jnp.tile(t, (1, reps)) for t in (cos_t, sin_lo, sin_hi))


def kernel(x, p, positions, ffn1_w_gate, ffn1_w_up, ffn1_w_down, ln1_g, ln1_b, w_in, ssm_a_re, ssm_a_im, ssm_b_re, ssm_b_im, ssm_c_re, ssm_c_im, ssm_log_dt, ssm_d, ssm_w_glu, w_branch_ssm, lambda_q1, lambda_k1, lambda_q2, lambda_k2, attn_subln_g, w_branch_attn, w_out, ln2_g, ln2_b, ffn2_w_gate, ffn2_w_up, ffn2_w_down, ln3_g, ln3_b, ple_w_gate, ple_w_proj, ln4_g, ln4_b):
    bsz, length, d = x.shape
    assert bsz == 1
    xs = x.reshape(length, d)
    cos, slo, shi = _rope_tables(positions[0])
    tri = jnp.kron(jnp.eye(SSM_ROWS // SSM_CHUNK, dtype=F32),
                   jnp.tril(jnp.ones((SSM_CHUNK, SSM_CHUNK), F32))).astype(BF16)
    row = lambda a: a.reshape(1, -1)
    for i in range(DEPTH):
        lam_init = 0.8 - 0.6 * math.exp(-0.3 * i)
        lam = (jnp.exp(jnp.sum(lambda_q1[i] * lambda_k1[i])) - jnp.exp(jnp.sum(lambda_q2[i] * lambda_k2[i]))
               + lam_init).reshape(1, 1)
        xs, xb = _ffn_ln(xs, ffn1_w_gate[i].astype(BF16), ffn1_w_up[i].astype(BF16),
                         (0.5 * ffn1_w_down[i]).astype(BF16), row(ln1_g[i]), row(ln1_b[i]), emit_bf16=True)
        u, qt, k, vt, sg = _inproj(xb, w_in[i].astype(BF16), cos, slo, shi)
        tables = _ssm_tables(ssm_a_re[i], ssm_a_im[i], ssm_b_re[i], ssm_b_im[i],
                             ssm_c_re[i], ssm_c_im[i], ssm_log_dt[i])
        ys = _ssm(u, *tables, row(ssm_d[i]), tri)
        attn = _attention(qt, k, vt, lam, row(attn_subln_g[i]), lam_init)
        xs = _mix(xs, ys, attn, sg, ssm_w_glu[i].astype(BF16), w_branch_ssm[i].astype(BF16),
                  w_branch_attn[i].astype(BF16), w_out[i].astype(BF16), row(ln2_g[i]), row(ln2_b[i]))
        xs = _ffn_ln(xs, ffn2_w_gate[i].astype(BF16), ffn2_w_up[i].astype(BF16),
                     (0.5 * ffn2_w_down[i]).astype(BF16), row(ln3_g[i]), row(ln3_b[i]), emit_bf16=False)
        xs = _ple(xs, p[i, 0], ple_w_gate[i].astype(BF16), ple_w_proj[i].astype(BF16),
                  row(ln4_g[i]), row(ln4_b[i]))
    return xs.reshape(bsz, length, d)
```

```python
import functools
import math

import jax
import jax.numpy as jnp
from jax import lax
from jax.experimental import pallas as pl
from jax.experimental.pallas import tpu as pltpu

F32 = jnp.float32
BF16 = jnp.bfloat16

DEPTH = 2
SSM_WIDTH = 1024
SSM_GROUP = 16
SSM_GROUPS = SSM_WIDTH // SSM_GROUP
SSM_STATE = 64
N_HEADS = 8
HEAD_DIM = 64
QK_WIDTH = N_HEADS * 2 * HEAD_DIM
V_WIDTH = N_HEADS * 2 * HEAD_DIM
ROT_DIM = HEAD_DIM // 4
ROPE_THETA = 500000.0
LN_EPS = 1e-5
RMS_EPS = 1e-5
NEG_INF = -1e30
ALPHA = (2 * DEPTH) ** 0.25
Q_SCALE = HEAD_DIM ** -0.5 * math.log2(math.e)

LANES = 128
BF16_ROWS = 16
VT_ROWS = 2 * HEAD_DIM + BF16_ROWS
VMEM_LIMIT_BYTES = 56 * 1024 * 1024
CAST_BLOCK_BYTES = 6 * 1024 * 1024

FFN_ROWS = 512
FFN_COLS = 512
PROJ_ROWS = 1024
PROJ_COLS = 1024
PROJ_CHUNK = 256
SSM_CHUNK = 128
SSM_ROWS = 256
SSM_SLABS = 4
ATTN_Q = 1024
ATTN_K = 1024
MIX_ROWS = 256
PLE_ROWS = 512


def _sigmoid(x):
    return 1.0 / (1.0 + jnp.exp(-x))


def _layer_norm(y, g, b):
    mu = jnp.mean(y, axis=-1, keepdims=True)
    yc = y - mu
    var = jnp.mean(yc * yc, axis=-1, keepdims=True)
    return yc * lax.rsqrt(var + LN_EPS) * g + b


def _params(*semantics):
    return pltpu.CompilerParams(dimension_semantics=semantics, vmem_limit_bytes=VMEM_LIMIT_BYTES)


def _cast_kernel(w_ref, o_ref, *, scale):
    w = w_ref[...]
    if scale != 1.0:
        w = w * scale
    o_ref[...] = w.astype(BF16)


def _to_bf16(w, scale=1.0):
    layers, r, c = w.shape
    fits = [br for br in range(BF16_ROWS, r + 1, BF16_ROWS) if r % br == 0 and br * c * 4 <= CAST_BLOCK_BYTES]
    br = max(fits) if fits else r
    spec = pl.BlockSpec((None, br, c), lambda l, i: (l, i, 0))
    return pl.pallas_call(
        functools.partial(_cast_kernel, scale=scale),
        grid=(layers, r // br),
        in_specs=[spec],
        out_specs=spec,
        out_shape=jax.ShapeDtypeStruct(w.shape, BF16),
        compiler_params=_params("parallel", "parallel"),
        name="cast_bf16",
    )(w)


def _layer_block(layer, shape, index, **kwargs):
    return pl.BlockSpec((None,) + shape, lambda *g: (layer,) + tuple(index(*g)), **kwargs)


def _ffn_ln_kernel(x_ref, wg_ref, wu_ref, wd_ref, g_ref, b_ref, *rest, emit_bf16):
    if emit_bf16:
        o_ref, ob_ref, xb_ref, acc_ref = rest
    else:
        o_ref, xb_ref, acc_ref = rest
    j = pl.program_id(1)

    @pl.when(j == 0)
    def _():
        xb_ref[...] = x_ref[...].astype(BF16)
        acc_ref[...] = jnp.zeros_like(acc_ref)

    xb = xb_ref[...]
    hg = jnp.dot(xb, wg_ref[...], preferred_element_type=F32)
    hu = jnp.dot(xb, wu_ref[...], preferred_element_type=F32)
    act = (hg * _sigmoid(hg) * hu).astype(BF16)
    acc_ref[...] += jnp.dot(act, wd_ref[...], preferred_element_type=F32)

    @pl.when(j == pl.num_programs(1) - 1)
    def _():
        y = _layer_norm(ALPHA * x_ref[...] + acc_ref[...], g_ref[...], b_ref[...])
        o_ref[...] = y
        if emit_bf16:
            ob_ref[...] = y.astype(BF16)


def _ffn_ln(x, wg, wu, wd, g, b, layer, emit_bf16):
    n, d = x.shape
    f = wg.shape[2]
    tm = min(FFN_ROWS, n)
    tf = FFN_COLS
    rows = pl.BlockSpec((tm, d), lambda i, j: (i, 0))
    return pl.pallas_call(
        functools.partial(_ffn_ln_kernel, emit_bf16=emit_bf16),
        grid=(n // tm, f // tf),
        in_specs=[
            rows,
            _layer_block(layer, (d, tf), lambda i, j: (0, j)),
            _layer_block(layer, (d, tf), lambda i, j: (0, j)),
            _layer_block(layer, (tf, d), lambda i, j: (j, 0)),
            pl.BlockSpec((1, d), lambda i, j: (0, 0)),
            pl.BlockSpec((1, d), lambda i, j: (0, 0)),
        ],
        out_specs=[rows, rows] if emit_bf16 else rows,
        out_shape=([jax.ShapeDtypeStruct((n, d), F32), jax.ShapeDtypeStruct((n, d), BF16)] if emit_bf16
                   else jax.ShapeDtypeStruct((n, d), F32)),
        scratch_shapes=[pltpu.VMEM((tm, d), BF16), pltpu.VMEM((tm, d), F32)],
        compiler_params=_params("parallel", "arbitrary"),
        name="ffn_ln",
    )(x, wg, wu, wd, g, b)


def _rotary(h, cos, sin_lo, sin_hi):
    half = ROT_DIM // 2
    return (h * cos + pltpu.roll(h, half, 1) * sin_hi
            + pltpu.roll(h, LANES - half, 1) * sin_lo)


def _inproj_kernel(xb_ref, w_ref, cos_ref, slo_ref, shi_ref, u_ref, qt_ref, k_ref, vt_ref, g_ref):
    j = pl.program_id(1)

    def for_chunks(store):
        xb = xb_ref[...]
        for c in range(w_ref.shape[1] // PROJ_CHUNK):
            cols = slice(c * PROJ_CHUNK, (c + 1) * PROJ_CHUNK)
            h = jnp.dot(xb, w_ref[:, cols], preferred_element_type=F32)
            for s in range(PROJ_CHUNK // LANES):
                lo = c * PROJ_CHUNK + s * LANES
                store(slice(lo, lo + LANES), h[:, s * LANES:(s + 1) * LANES])

    def rope(h):
        return _rotary(h, cos_ref[...], slo_ref[...], shi_ref[...])

    @pl.when(j == 0)
    def _():
        def store(cols, h):
            u_ref[:, cols] = h
        for_chunks(store)

    @pl.when(j == 1)
    def _():
        def store(cols, h):
            qt_ref[cols, :] = (rope(h) * Q_SCALE).T.astype(BF16)
        for_chunks(store)

    @pl.when(j == 2)
    def _():
        def store(cols, h):
            k_ref[:, cols] = rope(h).astype(BF16)
        for_chunks(store)

    @pl.when(j == 3)
    def _():
        def store(cols, h):
            base = cols.start // LANES * VT_ROWS
            vt_ref[base:base + LANES, :] = h.T.astype(BF16)
            vt_ref[base + LANES:base + VT_ROWS, :] = jnp.ones((VT_ROWS - LANES, h.shape[0]), BF16)
        for_chunks(store)

    @pl.when(j >= 4)
    def _():
        def store(cols, h):
            g_ref[:, cols] = _sigmoid(h).astype(BF16)
        for_chunks(store)


def _inproj(x, w, cos, slo, shi, layer):
    n, d = x.shape
    tm = min(PROJ_ROWS, n)
    tn = PROJ_COLS
    width = w.shape[2]
    n_gate = (width - SSM_WIDTH - 2 * QK_WIDTH - V_WIDTH) // tn
    assert SSM_WIDTH == tn and QK_WIDTH == tn and V_WIDTH == tn
    tab = pl.BlockSpec((tm, LANES), lambda i, j: (i, 0))
    return pl.pallas_call(
        _inproj_kernel,
        grid=(n // tm, width // tn),
        in_specs=[
            pl.BlockSpec((tm, d), lambda i, j: (i, 0)),
            _layer_block(layer, (d, tn), lambda i, j: (0, j)),
            tab, tab, tab,
        ],
        out_specs=[
            pl.BlockSpec((tm, tn), lambda i, j: (i, 0)),
            pl.BlockSpec((tn, tm), lambda i, j: (0, i)),
            pl.BlockSpec((tm, tn), lambda i, j: (i, 0)),
            pl.BlockSpec((N_HEADS * VT_ROWS, tm), lambda i, j: (0, i)),
            pl.BlockSpec((tm, tn), lambda i, j: (i, jnp.clip(j - 4, 0, n_gate - 1))),
        ],
        out_shape=[
            jax.ShapeDtypeStruct((n, SSM_WIDTH), F32),
            jax.ShapeDtypeStruct((QK_WIDTH, n), BF16),
            jax.ShapeDtypeStruct((n, QK_WIDTH), BF16),
            jax.ShapeDtypeStruct((N_HEADS * VT_ROWS, n), BF16),
            jax.ShapeDtypeStruct((n, n_gate * tn), BF16),
        ],
        compiler_params=_params("parallel", "arbitrary"),
        name="inproj",
    )(x, w, cos, slo, shi)


def _ssm_kernel(u_ref, bbd_ref, cbd_ref, pn_re_ref, pn_im_ref, pp_re_ref, pp_im_ref,
                lb_re_ref, lb_im_ref, d_ref, tri_ref, y_ref, car_re_ref, car_im_ref):
    @pl.when(pl.program_id(0) == 0)
    def _():
        car_re_ref[...] = jnp.zeros_like(car_re_ref)
        car_im_ref[...] = jnp.zeros_like(car_im_ref)

    t_len = pn_re_ref.shape[0]
    n_sub = u_ref.shape[0] // t_len
    slab_in = u_ref.shape[1] // SSM_SLABS
    slab_st = pn_re_ref.shape[1] // SSM_SLABS
    u = u_ref[...]
    ub = u.astype(BF16)
    tri = tri_ref[...]
    for s in range(SSM_SLABS):
        cin = slice(s * slab_in, (s + 1) * slab_in)
        cst = slice(s * slab_st, (s + 1) * slab_st)
        bu = jnp.dot(ub[:, cin], bbd_ref[s], preferred_element_type=F32)
        nr, ni = pn_re_ref[:, cst], pn_im_ref[:, cst]
        z = []
        for k in range(n_sub):
            br = bu[k * t_len:(k + 1) * t_len, :slab_st]
            bi = bu[k * t_len:(k + 1) * t_len, slab_st:]
            z.append(jnp.concatenate([br * nr - bi * ni, br * ni + bi * nr], axis=1).astype(BF16))
        cs = jnp.dot(tri, jnp.concatenate(z, axis=0), preferred_element_type=F32)
        lr, li = lb_re_ref[:, cst], lb_im_ref[:, cst]
        pr, pi = pp_re_ref[:, cst], pp_im_ref[:, cst]
        kr, ki = car_re_ref[:, cst], car_im_ref[:, cst]
        states = []
        for k in range(n_sub):
            cr = cs[k * t_len:(k + 1) * t_len, :slab_st] + (lr * kr - li * ki)
            ci = cs[k * t_len:(k + 1) * t_len, slab_st:] + (lr * ki + li * kr)
            sr = cr * pr - ci * pi
            si = cr * pi + ci * pr
            kr, ki = sr[t_len - 1:t_len, :], si[t_len - 1:t_len, :]
            states.append(jnp.concatenate([sr, si], axis=1).astype(BF16))
        car_re_ref[:, cst] = kr
        car_im_ref[:, cst] = ki
        y = jnp.dot(jnp.concatenate(states, axis=0), cbd_ref[s], preferred_element_type=F32)
        y_ref[:, cin] = y + d_ref[:, cin] * u[:, cin]


def _ssm(u, bbd, cbd, pn_re, pn_im, pp_re, pp_im, lb_re, lb_im, d_skip, tri):
    n, w = u.shape
    t = SSM_CHUNK
    rows = tri.shape[0]
    ns = pn_re.shape[1]
    const2 = lambda shape: pl.BlockSpec(shape, lambda i: (0, 0))
    const3 = lambda shape: pl.BlockSpec(shape, lambda i: (0, 0, 0))
    return pl.pallas_call(
        _ssm_kernel,
        grid=(n // rows,),
        in_specs=[
            pl.BlockSpec((rows, w), lambda i: (i, 0)),
            const3(bbd.shape), const3(cbd.shape),
            const2((t, ns)), const2((t, ns)), const2((t, ns)), const2((t, ns)),
            const2((1, ns)), const2((1, ns)), const2((1, w)), const2((rows, rows)),
        ],
        out_specs=pl.BlockSpec((rows, w), lambda i: (i, 0)),
        out_shape=jax.ShapeDtypeStruct((n, w), F32),
        scratch_shapes=[pltpu.VMEM((1, ns), F32), pltpu.VMEM((1, ns), F32)],
        compiler_params=_params("arbitrary"),
        name="ssm_scan",
    )(u, bbd, cbd, pn_re, pn_im, pp_re, pp_im, lb_re, lb_im, d_skip, tri)


def _complex_powers(lr, li, count):
    pr, pi = jnp.ones_like(lr), jnp.zeros_like(li)
    cr, ci = lr, li
    rows = 1
    while rows < count:
        pr, pi = (jnp.concatenate([pr, pr * cr - pi * ci], axis=0),
                  jnp.concatenate([pi, pr * ci + pi * cr], axis=0))
        cr, ci = cr * cr - ci * ci, 2.0 * cr * ci
        rows *= 2
    return pr[:count], pi[:count]


def _ssm_tables(a_re, a_im, b_re, b_im, c_re, c_im, log_dt):
    g, p = a_re.shape
    h = b_re.shape[2]
    dt = jnp.exp(log_dt)[:, None]
    mag = jnp.exp(a_re * dt)
    lb_re = mag * jnp.cos(a_im * dt)
    lb_im = mag * jnp.sin(a_im * dt)
    nr, ni = lb_re - 1.0, lb_im
    den = a_re * a_re + a_im * a_im
    coef_re = (nr * a_re + ni * a_im) / den
    coef_im = (ni * a_re - nr * a_im) / den
    bb_re = coef_re[:, :, None] * b_re - coef_im[:, :, None] * b_im
    bb_im = coef_re[:, :, None] * b_im + coef_im[:, :, None] * b_re
    gs = g // SSM_SLABS
    eye = jnp.eye(gs, dtype=F32)

    def in_blocks(bb):
        t = bb.reshape(SSM_SLABS, gs, p, h).transpose(0, 1, 3, 2)
        return jnp.einsum("ab,sbhp->sahbp", eye, t).reshape(SSM_SLABS, gs * h, gs * p)

    def out_blocks(cc):
        t = cc.reshape(SSM_SLABS, gs, h, p).transpose(0, 1, 3, 2)
        return jnp.einsum("ab,sbph->sbpah", eye, t).reshape(SSM_SLABS, gs * p, gs * h)

    bbd = jnp.concatenate([in_blocks(bb_re), in_blocks(bb_im)], axis=2).astype(BF16)
    cbd = jnp.concatenate([out_blocks(c_re), -out_blocks(c_im)], axis=1).astype(BF16)
    lr, li = lb_re.reshape(1, g * p), lb_im.reshape(1, g * p)
    inv_den = lr * lr + li * li
    pp_re, pp_im = _complex_powers(lr, li, SSM_CHUNK)
    pn_re, pn_im = _complex_powers(lr / inv_den, -li / inv_den, SSM_CHUNK)
    return bbd, cbd, pn_re, pn_im, pp_re, pp_im, lr, li


def _attn_kernel(lam_ref, qt_ref, k_ref, vt_ref, g_ref, o_ref,
                 qz_ref, sa_ref, sb_ref, mxa_ref, mxb_ref, m_ref, acc_ref, *, tk, out_scale):
    i = pl.program_id(1)
    tq = qt_ref.shape[1]
    assert tq == tk
    qt = qt_ref[...]
    comp = lax.broadcasted_iota(jnp.int32, qt.shape, 0) < HEAD_DIM
    zero = jnp.zeros_like(qt)
    qz_ref[0] = jnp.where(comp, qt, zero)
    qz_ref[1] = jnp.where(comp, zero, qt)
    m_ref[...] = jnp.full_like(m_ref, NEG_INF)
    acc_ref[...] = jnp.zeros_like(acc_ref)

    def scores(j, s_ref, mx_ref, masked):
        start = pl.multiple_of(j * tk, tk)
        kb = k_ref[pl.ds(start, tk), :]
        for c in range(2):
            s = jnp.dot(kb, qz_ref[c], preferred_element_type=F32)
            if masked:
                key = lax.broadcasted_iota(jnp.int32, s.shape, 0)
                qry = lax.broadcasted_iota(jnp.int32, s.shape, 1)
                s = jnp.where(key <= qry, s, NEG_INF)
            s_ref[c] = s
            mx_ref[c] = jnp.max(s, axis=0, keepdims=True)

    def update(j, s_ref, mx_ref):
        start = pl.multiple_of(j * tk, tk)
        vtb = vt_ref[:, pl.ds(start, tk)]
        for c in range(2):
            m_prev = m_ref[c]
            m_new = jnp.maximum(m_prev, mx_ref[c])
            alpha = jnp.exp2(m_prev - m_new)
            p = jnp.exp2(s_ref[c] - m_new).astype(BF16)
            acc_ref[c] = alpha * acc_ref[c] + jnp.dot(vtb, p, preferred_element_type=F32)
            m_ref[c] = m_new

    @pl.when(i == 0)
    def _():
        scores(0, sa_ref, mxa_ref, True)
        update(0, sa_ref, mxa_ref)

    @pl.when(i > 0)
    def _():
        scores(0, sa_ref, mxa_ref, False)

    def pair(jj, carry):
        a = 2 * jj
        scores(a + 1, sb_ref, mxb_ref, False)
        update(a, sa_ref, mxa_ref)
        scores(a + 2, sa_ref, mxa_ref, False)
        update(a + 1, sb_ref, mxb_ref)
        return carry

    n_pairs = lax.shift_right_logical(jnp.maximum(i - 1, 0), 1)
    lax.fori_loop(0, n_pairs, pair, 0)
    first = 2 * n_pairs

    @pl.when(i % 2 == 1)
    def _():
        scores(i, sb_ref, mxb_ref, True)
        update(first, sa_ref, mxa_ref)
        update(i, sb_ref, mxb_ref)

    @pl.when((i % 2 == 0) & (i > 0))
    def _():
        scores(first + 1, sb_ref, mxb_ref, False)
        update(first, sa_ref, mxa_ref)
        scores(i, sa_ref, mxa_ref, True)
        update(first + 1, sb_ref, mxb_ref)
        update(i, sa_ref, mxa_ref)

    hw = 2 * HEAD_DIM
    o1 = acc_ref[0, :hw, :] / acc_ref[0, hw:hw + 1, :]
    o2 = acc_ref[1, :hw, :] / acc_ref[1, hw:hw + 1, :]
    ot = o1 - lam_ref[0, 0] * o2
    ms = jnp.mean(ot * ot, axis=0, keepdims=True)
    o = (ot * lax.rsqrt(ms + RMS_EPS)).T
    o_ref[...] = (o * g_ref[...] * out_scale).astype(o_ref.dtype)


def _attention(qt, k, vt, lam, subln_g, lam_init):
    n = k.shape[0]
    tq = min(ATTN_Q, n)
    tk = min(ATTN_K, n)
    hw = 2 * HEAD_DIM
    kernel = functools.partial(_attn_kernel, tk=tk, out_scale=1.0 - lam_init)
    return pl.pallas_call(
        kernel,
        grid=(N_HEADS, n // tq),
        in_specs=[
            pl.BlockSpec(memory_space=pltpu.SMEM),
            pl.BlockSpec((hw, tq), lambda h, i: (h, i)),
            pl.BlockSpec((n, hw), lambda h, i: (0, h)),
            pl.BlockSpec((VT_ROWS, n), lambda h, i: (h, 0)),
            pl.BlockSpec((1, hw), lambda h, i: (0, 0)),
        ],
        out_specs=pl.BlockSpec((tq, hw), lambda h, i: (i, h)),
        out_shape=jax.ShapeDtypeStruct((n, V_WIDTH), BF16),
        scratch_shapes=[
            pltpu.VMEM((2, hw, tq), BF16),
            pltpu.VMEM((2, tk, tq), F32), pltpu.VMEM((2, tk, tq), F32),
            pltpu.VMEM((2, 1, tq), F32), pltpu.VMEM((2, 1, tq), F32),
            pltpu.VMEM((2, 1, tq), F32),
            pltpu.VMEM((2, VT_ROWS, tq), F32),
        ],
        compiler_params=_params("parallel", "arbitrary"),
        name="diff_attn",
    )(lam, qt, k, vt, subln_g)


def _mix_kernel(x_ref, ys_ref, at_ref, sg_ref, wglu_ref, wbs_ref, wba_ref, wout_ref,
                g_ref, b_ref, o_ref):
    d = x_ref.shape[1]
    y = ys_ref[...]
    z = 0.5 * y * (1.0 + lax.erf(y * (2.0 ** -0.5)))
    gate = _sigmoid(jnp.dot(z.astype(BF16), wglu_ref[...], preferred_element_type=F32))
    glu = (z * gate).astype(BF16)
    ya = jnp.dot(glu, wbs_ref[...], preferred_element_type=F32)
    yb = jnp.dot(at_ref[...], wba_ref[...], preferred_element_type=F32)
    sg = sg_ref[...].astype(F32)
    mix = (sg[:, :d] * ya + sg[:, d:] * yb).astype(BF16)
    out = ALPHA * x_ref[...] + jnp.dot(mix, wout_ref[...], preferred_element_type=F32)
    o_ref[...] = _layer_norm(out, g_ref[...], b_ref[...])


def _resident(shape):
    return pl.BlockSpec(shape, lambda i: (0,) * len(shape), pipeline_mode=pl.Buffered(1))


def _resident_layer(w, layer):
    return _layer_block(layer, w.shape[1:], lambda i: (0, 0), pipeline_mode=pl.Buffered(1))


def _mix(x, ys, attn, sg, wglu, wbs, wba, wout, g, b, layer):
    n, d = x.shape
    tm = min(MIX_ROWS, n)
    rows = lambda w: pl.BlockSpec((tm, w), lambda i: (i, 0))
    return pl.pallas_call(
        _mix_kernel,
        grid=(n // tm,),
        in_specs=[
            rows(d), rows(ys.shape[1]), rows(attn.shape[1]), rows(sg.shape[1]),
            _resident_layer(wglu, layer), _resident_layer(wbs, layer), _resident_layer(wba, layer),
            _resident_layer(wout, layer),
            _resident((1, d)), _resident((1, d)),
        ],
        out_specs=rows(d),
        out_shape=jax.ShapeDtypeStruct((n, d), F32),
        compiler_params=_params("parallel"),
        name="mixer_out",
    )(x, ys, attn, sg, wglu, wbs, wba, wout, g, b)


def _ple_kernel(x_ref, p_ref, wg_ref, wp_ref, g_ref, b_ref, o_ref):
    x = x_ref[...]
    gate = _sigmoid(jnp.dot(x.astype(BF16), wg_ref[...], preferred_element_type=F32))
    emb = jnp.dot(p_ref[...].astype(BF16), wp_ref[...], preferred_element_type=F32)
    o_ref[...] = _layer_norm(ALPHA * x + gate * emb, g_ref[...], b_ref[...])


def _ple(x, p, wg, wp, g, b, layer):
    n, d = x.shape
    tm = min(PLE_ROWS, n)
    rows = lambda w: pl.BlockSpec((tm, w), lambda i: (i, 0))
    return pl.pallas_call(
        _ple_kernel,
        grid=(n // tm,),
        in_specs=[rows(d), rows(p.shape[1]), _resident_layer(wg, layer), _resident_layer(wp, layer),
                  _resident((1, d)), _resident((1, d))],
        out_specs=rows(d),
        out_shape=jax.ShapeDtypeStruct((n, d), F32),
        compiler_params=_params("parallel"),
        name="ple_ln",
    )(x, p, wg, wp, g, b)


def _rope_tables(positions):
    half = ROT_DIM // 2
    inv_freq = ROPE_THETA ** (-jnp.arange(0, ROT_DIM, 2, dtype=F32) / ROT_DIM)
    ang = positions.astype(F32)[:, None] * inv_freq
    cos, sin = jnp.cos(ang), jnp.sin(ang)
    n = positions.shape[0]
    pad = jnp.zeros((n, HEAD_DIM - ROT_DIM), F32)
    zeros = jnp.zeros((n, half), F32)
    cos_t = jnp.concatenate([cos, cos, pad + 1.0], axis=1)
    sin_lo = jnp.concatenate([-sin, zeros, pad], axis=1)
    sin_hi = jnp.concatenate([zeros, sin, pad], axis=1)
    reps = LANES // HEAD_DIM
    return tuple(jnp.tile(t, (1, reps)) for t in (cos_t, sin_lo, sin_hi))


def kernel(x, p, positions, ffn1_w_gate, ffn1_w_up, ffn1_w_down, ln1_g, ln1_b, w_in, ssm_a_re, ssm_a_im, ssm_b_re, ssm_b_im, ssm_c_re, ssm_c_im, ssm_log_dt, ssm_d, ssm_w_glu, w_branch_ssm, lambda_q1, lambda_k1, lambda_q2, lambda_k2, attn_subln_g, w_branch_attn, w_out, ln2_g, ln2_b, ffn2_w_gate, ffn2_w_up, ffn2_w_down, ln3_g, ln3_b, ple_w_gate, ple_w_proj, ln4_g, ln4_b):
    bsz, length, d = x.shape
    assert bsz == 1
    xs = x.reshape(length, d)
    cos, slo, shi = _rope_tables(positions[0])
    tri = jnp.kron(jnp.eye(SSM_ROWS // SSM_CHUNK, dtype=F32),
                   jnp.tril(jnp.ones((SSM_CHUNK, SSM_CHUNK), F32))).astype(BF16)
    row = lambda a: a.reshape(1, -1)
    w_ffn1 = (_to_bf16(ffn1_w_gate), _to_bf16(ffn1_w_up), _to_bf16(ffn1_w_down, 0.5))
    w_ffn2 = (_to_bf16(ffn2_w_gate), _to_bf16(ffn2_w_up), _to_bf16(ffn2_w_down, 0.5))
    w_inb = _to_bf16(w_in)
    w_mix = (_to_bf16(ssm_w_glu), _to_bf16(w_branch_ssm), _to_bf16(w_branch_attn), _to_bf16(w_out))
    w_ple = (_to_bf16(ple_w_gate), _to_bf16(ple_w_proj))
    for i in range(DEPTH):
        lam_init = 0.8 - 0.6 * math.exp(-0.3 * i)
        lam = (jnp.exp(jnp.sum(lambda_q1[i] * lambda_k1[i])) - jnp.exp(jnp.sum(lambda_q2[i] * lambda_k2[i]))
               + lam_init).reshape(1, 1)
        xs, xb = _ffn_ln(xs, *w_ffn1, row(ln1_g[i]), row(ln1_b[i]), layer=i, emit_bf16=True)
        u, qt, k, vt, sg = _inproj(xb, w_inb, cos, slo, shi, layer=i)
        tables = _ssm_tables(ssm_a_re[i], ssm_a_im[i], ssm_b_re[i], ssm_b_im[i],
                             ssm_c_re[i], ssm_c_im[i], ssm_log_dt[i])
        ys = _ssm(u, *tables, row(ssm_d[i]), tri)
        attn = _attention(qt, k, vt, lam, row(attn_subln_g[i]), lam_init)
        xs = _mix(xs, ys, attn, sg, *w_mix, row(ln2_g[i]), row(ln2_b[i]), layer=i)
        xs = _ffn_ln(xs, *w_ffn2, row(ln3_g[i]), row(ln3_b[i]), layer=i, emit_bf16=False)
        xs = _ple(xs, p[i, 0], *w_ple, row(ln4_g[i]), row(ln4_b[i]), layer=i)
    return xs.reshape(bsz, length, d)
```

```python
import functools
import math

import jax
import jax.numpy as jnp
from jax import lax
from jax.experimental import pallas as pl
from jax.experimental.pallas import tpu as pltpu

F32 = jnp.float32
BF16 = jnp.bfloat16

DEPTH = 2
SSM_WIDTH = 1024
SSM_GROUP = 16
SSM_GROUPS = SSM_WIDTH // SSM_GROUP
SSM_STATE = 64
N_HEADS = 8
HEAD_DIM = 64
QK_WIDTH = N_HEADS * 2 * HEAD_DIM
V_WIDTH = N_HEADS * 2 * HEAD_DIM
ROT_DIM = HEAD_DIM // 4
ROPE_THETA = 500000.0
LN_EPS = 1e-5
RMS_EPS = 1e-5
NEG_INF = -1e30
ALPHA = (2 * DEPTH) ** 0.25
Q_SCALE = HEAD_DIM ** -0.5 * math.log2(math.e)

LANES = 128
BF16_ROWS = 16
VT_ROWS = 2 * HEAD_DIM + BF16_ROWS
VMEM_LIMIT_BYTES = 56 * 1024 * 1024
CAST_BLOCK_BYTES = 6 * 1024 * 1024

FFN_ROWS = 512
FFN_COLS = 512
PROJ_ROWS = 1024
PROJ_COLS = 1024
PROJ_CHUNK = 256
SSM_CHUNK = 128
SSM_ROWS = 256
SSM_SLABS = 4
ATTN_Q = 1024
ATTN_K = 1024
MIX_ROWS = 256
PLE_ROWS = 512


def _sigmoid(x):
    return 1.0 / (1.0 + jnp.exp(-x))


def _layer_norm(y, g, b):
    mu = jnp.mean(y, axis=-1, keepdims=True)
    yc = y - mu
    var = jnp.mean(yc * yc, axis=-1, keepdims=True)
    return yc * lax.rsqrt(var + LN_EPS) * g + b


def _params(*semantics):
    return pltpu.CompilerParams(dimension_semantics=semantics, vmem_limit_bytes=VMEM_LIMIT_BYTES)


def _cast_kernel(w_ref, o_ref, *, scale):
    w = w_ref[...]
    if scale != 1.0:
        w = w * scale
    if len(o_ref.shape) == 2:
        o_ref[...] = w.astype(BF16)
    else:
        tc = o_ref.shape[2]
        for t in range(o_ref.shape[0]):
            o_ref[t] = w[:, t * tc:(t + 1) * tc].astype(BF16)


def _to_bf16(w, scale=1.0, col_tile=None):
    layers, r, c = w.shape
    fits = [br for br in range(BF16_ROWS, r + 1, BF16_ROWS) if r % br == 0 and br * c * 4 <= CAST_BLOCK_BYTES]
    br = max(fits) if fits else r
    in_spec = pl.BlockSpec((None, br, c), lambda l, i: (l, i, 0))
    if col_tile is None:
        out_spec, out_shape = in_spec, w.shape
    else:
        tiles = c // col_tile
        out_spec = pl.BlockSpec((None, tiles, br, col_tile), lambda l, i: (l, 0, i, 0))
        out_shape = (layers, tiles, r, col_tile)
    return pl.pallas_call(
        functools.partial(_cast_kernel, scale=scale),
        grid=(layers, r // br),
        in_specs=[in_spec],
        out_specs=out_spec,
        out_shape=jax.ShapeDtypeStruct(out_shape, BF16),
        compiler_params=_params("parallel", "parallel"),
        name="cast_bf16",
    )(w)


def _layer_block(layer, shape, index, **kwargs):
    return pl.BlockSpec((None,) + shape, lambda *g: (layer,) + tuple(index(*g)), **kwargs)


def _ffn_ln_kernel(x_ref, wg_ref, wu_ref, wd_ref, g_ref, b_ref, *rest, emit_bf16):
    if emit_bf16:
        o_ref, ob_ref, xb_ref, acc_ref = rest
    else:
        o_ref, xb_ref, acc_ref = rest
    j = pl.program_id(1)

    @pl.when(j == 0)
    def _():
        xb_ref[...] = x_ref[...].astype(BF16)
        acc_ref[...] = jnp.zeros_like(acc_ref)

    xb = xb_ref[...]
    hg = jnp.dot(xb, wg_ref[...], preferred_element_type=F32)
    hu = jnp.dot(xb, wu_ref[...], preferred_element_type=F32)
    act = (hg * _sigmoid(hg) * hu).astype(BF16)
    acc_ref[...] += jnp.dot(act, wd_ref[...], preferred_element_type=F32)

    @pl.when(j == pl.num_programs(1) - 1)
    def _():
        y = _layer_norm(ALPHA * x_ref[...] + acc_ref[...], g_ref[...], b_ref[...])
        o_ref[...] = y
        if emit_bf16:
            ob_ref[...] = y.astype(BF16)


def _ffn_ln(x, wg, wu, wd, g, b, layer, emit_bf16):
    n, d = x.shape
    n_tiles, tf = wg.shape[1], wg.shape[3]
    tm = min(FFN_ROWS, n)
    rows = pl.BlockSpec((tm, d), lambda i, j: (i, 0))
    return pl.pallas_call(
        functools.partial(_ffn_ln_kernel, emit_bf16=emit_bf16),
        grid=(n // tm, n_tiles),
        in_specs=[
            rows,
            _layer_block(layer, (None, d, tf), lambda i, j: (j, 0, 0)),
            _layer_block(layer, (None, d, tf), lambda i, j: (j, 0, 0)),
            _layer_block(layer, (tf, d), lambda i, j: (j, 0)),
            pl.BlockSpec((1, d), lambda i, j: (0, 0)),
            pl.BlockSpec((1, d), lambda i, j: (0, 0)),
        ],
        out_specs=[rows, rows] if emit_bf16 else rows,
        out_shape=([jax.ShapeDtypeStruct((n, d), F32), jax.ShapeDtypeStruct((n, d), BF16)] if emit_bf16
                   else jax.ShapeDtypeStruct((n, d), F32)),
        scratch_shapes=[pltpu.VMEM((tm, d), BF16), pltpu.VMEM((tm, d), F32)],
        compiler_params=_params("parallel", "arbitrary"),
        name="ffn_ln",
    )(x, wg, wu, wd, g, b)


def _rotary(h, cos, sin_lo, sin_hi):
    half = ROT_DIM // 2
    return (h * cos + pltpu.roll(h, half, 1) * sin_hi
            + pltpu.roll(h, LANES - half, 1) * sin_lo)


def _inproj_kernel(xb_ref, w_ref, cos_ref, slo_ref, shi_ref, u_ref, qt_ref, k_ref, vt_ref, g_ref):
    j = pl.program_id(1)

    def for_chunks(store):
        xb = xb_ref[...]
        for c in range(w_ref.shape[1] // PROJ_CHUNK):
            cols = slice(c * PROJ_CHUNK, (c + 1) * PROJ_CHUNK)
            h = jnp.dot(xb, w_ref[:, cols], preferred_element_type=F32)
            for s in range(PROJ_CHUNK // LANES):
                lo = c * PROJ_CHUNK + s * LANES
                store(slice(lo, lo + LANES), h[:, s * LANES:(s + 1) * LANES])

    def rope(h):
        return _rotary(h, cos_ref[...], slo_ref[...], shi_ref[...])

    @pl.when(j == 0)
    def _():
        def store(cols, h):
            u_ref[:, cols] = h
        for_chunks(store)

    @pl.when(j == 1)
    def _():
        def store(cols, h):
            qt_ref[cols, :] = (rope(h) * Q_SCALE).T.astype(BF16)
        for_chunks(store)

    @pl.when(j == 2)
    def _():
        def store(cols, h):
            k_ref[:, cols] = rope(h).astype(BF16)
        for_chunks(store)

    @pl.when(j == 3)
    def _():
        def store(cols, h):
            base = cols.start // LANES * VT_ROWS
            vt_ref[base:base + LANES, :] = h.T.astype(BF16)
            vt_ref[base + LANES:base + VT_ROWS, :] = jnp.ones((VT_ROWS - LANES, h.shape[0]), BF16)
        for_chunks(store)

    @pl.when(j >= 4)
    def _():
        def store(cols, h):
            g_ref[:, cols] = _sigmoid(h).astype(BF16)
        for_chunks(store)


def _inproj(x, w, cos, slo, shi, layer):
    n, d = x.shape
    tm = min(PROJ_ROWS, n)
    tn = w.shape[3]
    width = w.shape[1] * tn
    n_gate = (width - SSM_WIDTH - 2 * QK_WIDTH - V_WIDTH) // tn
    assert SSM_WIDTH == tn and QK_WIDTH == tn and V_WIDTH == tn
    tab = pl.BlockSpec((tm, LANES), lambda i, j: (i, 0))
    return pl.pallas_call(
        _inproj_kernel,
        grid=(n // tm, width // tn),
        in_specs=[
            pl.BlockSpec((tm, d), lambda i, j: (i, 0)),
            _layer_block(layer, (None, d, tn), lambda i, j: (j, 0, 0)),
            tab, tab, tab,
        ],
        out_specs=[
            pl.BlockSpec((tm, tn), lambda i, j: (i, 0)),
            pl.BlockSpec((tn, tm), lambda i, j: (0, i)),
            pl.BlockSpec((tm, tn), lambda i, j: (i, 0)),
            pl.BlockSpec((N_HEADS * VT_ROWS, tm), lambda i, j: (0, i)),
            pl.BlockSpec((tm, tn), lambda i, j: (i, jnp.clip(j - 4, 0, n_gate - 1))),
        ],
        out_shape=[
            jax.ShapeDtypeStruct((n, SSM_WIDTH), F32),
            jax.ShapeDtypeStruct((QK_WIDTH, n), BF16),
            jax.ShapeDtypeStruct((n, QK_WIDTH), BF16),
            jax.ShapeDtypeStruct((N_HEADS * VT_ROWS, n), BF16),
            jax.ShapeDtypeStruct((n, n_gate * tn), BF16),
        ],
        compiler_params=_params("parallel", "arbitrary"),
        name="inproj",
    )(x, w, cos, slo, shi)


def _ssm_kernel(u_ref, bbd_ref, cbd_ref, pn_re_ref, pn_im_ref, pp_re_ref, pp_im_ref,
                lb_re_ref, lb_im_ref, d_ref, tri_ref, y_ref, car_re_ref, car_im_ref):
    @pl.when(pl.program_id(0) == 0)
    def _():
        car_re_ref[...] = jnp.zeros_like(car_re_ref)
        car_im_ref[...] = jnp.zeros_like(car_im_ref)

    t_len = pn_re_ref.shape[0]
    n_sub = u_ref.shape[0] // t_len
    slab_in = u_ref.shape[1] // SSM_SLABS
    slab_st = pn_re_ref.shape[1] // SSM_SLABS
    u = u_ref[...]
    ub = u.astype(BF16)
    tri = tri_ref[...]
    for s in range(SSM_SLABS):
        cin = slice(s * slab_in, (s + 1) * slab_in)
        cst = slice(s * slab_st, (s + 1) * slab_st)
        bu = jnp.dot(ub[:, cin], bbd_ref[s], preferred_element_type=F32)
        nr, ni = pn_re_ref[:, cst], pn_im_ref[:, cst]
        z = []
        for k in range(n_sub):
            br = bu[k * t_len:(k + 1) * t_len, :slab_st]
            bi = bu[k * t_len:(k + 1) * t_len, slab_st:]
            z.append(jnp.concatenate([br * nr - bi * ni, br * ni + bi * nr], axis=1).astype(BF16))
        cs = jnp.dot(tri, jnp.concatenate(z, axis=0), preferred_element_type=F32)
        lr, li = lb_re_ref[:, cst], lb_im_ref[:, cst]
        pr, pi = pp_re_ref[:, cst], pp_im_ref[:, cst]
        kr, ki = car_re_ref[:, cst], car_im_ref[:, cst]
        states = []
        for k in range(n_sub):
            cr = cs[k * t_len:(k + 1) * t_len, :slab_st] + (lr * kr - li * ki)
            ci = cs[k * t_len:(k + 1) * t_len, slab_st:] + (lr * ki + li * kr)
            sr = cr * pr - ci * pi
            si = cr * pi + ci * pr
            kr, ki = sr[t_len - 1:t_len, :], si[t_len - 1:t_len, :]
            states.append(jnp.concatenate([sr, si], axis=1).astype(BF16))
        car_re_ref[:, cst] = kr
        car_im_ref[:, cst] = ki
        y = jnp.dot(jnp.concatenate(states, axis=0), cbd_ref[s], preferred_element_type=F32)
        y_ref[:, cin] = y + d_ref[:, cin] * u[:, cin]


def _ssm(u, tables, d_skip, tri, layer):
    n, w = u.shape
    rows = tri.shape[0]
    ns = tables[2].shape[2]
    const2 = lambda shape: pl.BlockSpec(shape, lambda i: (0, 0))
    table = lambda a: _layer_block(layer, a.shape[1:], lambda i: (0,) * (a.ndim - 1))
    return pl.pallas_call(
        _ssm_kernel,
        grid=(n // rows,),
        in_specs=[
            pl.BlockSpec((rows, w), lambda i: (i, 0)),
            *[table(a) for a in tables],
            const2((1, w)), const2((rows, rows)),
        ],
        out_specs=pl.BlockSpec((rows, w), lambda i: (i, 0)),
        out_shape=jax.ShapeDtypeStruct((n, w), F32),
        scratch_shapes=[pltpu.VMEM((1, ns), F32), pltpu.VMEM((1, ns), F32)],
        compiler_params=_params("arbitrary"),
        name="ssm_scan",
    )(u, *tables, d_skip, tri)


def _complex_powers(lr, li, count):
    pr, pi = jnp.ones_like(lr), jnp.zeros_like(li)
    cr, ci = lr, li
    rows = 1
    while rows < count:
        pr, pi = (jnp.concatenate([pr, pr * cr - pi * ci], axis=0),
                  jnp.concatenate([pi, pr * ci + pi * cr], axis=0))
        cr, ci = cr * cr - ci * ci, 2.0 * cr * ci
        rows *= 2
    return pr[:count], pi[:count]


def _ssm_tables(a_re, a_im, b_re, b_im, c_re, c_im, log_dt):
    g, p = a_re.shape
    h = b_re.shape[2]
    dt = jnp.exp(log_dt)[:, None]
    mag = jnp.exp(a_re * dt)
    lb_re = mag * jnp.cos(a_im * dt)
    lb_im = mag * jnp.sin(a_im * dt)
    nr, ni = lb_re - 1.0, lb_im
    den = a_re * a_re + a_im * a_im
    coef_re = (nr * a_re + ni * a_im) / den
    coef_im = (ni * a_re - nr * a_im) / den
    bb_re = coef_re[:, :, None] * b_re - coef_im[:, :, None] * b_im
    bb_im = coef_re[:, :, None] * b_im + coef_im[:, :, None] * b_re
    gs = g // SSM_SLABS
    eye = jnp.eye(gs, dtype=F32)

    def in_blocks(bb):
        t = bb.reshape(SSM_SLABS, gs, p, h).transpose(0, 1, 3, 2)
        return jnp.einsum("ab,sbhp->sahbp", eye, t).reshape(SSM_SLABS, gs * h, gs * p)

    def out_blocks(cc):
        t = cc.reshape(SSM_SLABS, gs, h, p).transpose(0, 1, 3, 2)
        return jnp.einsum("ab,sbph->sbpah", eye, t).reshape(SSM_SLABS, gs * p, gs * h)

    bbd = jnp.concatenate([in_blocks(bb_re), in_blocks(bb_im)], axis=2).astype(BF16)
    cbd = jnp.concatenate([out_blocks(c_re), -out_blocks(c_im)], axis=1).astype(BF16)
    lr, li = lb_re.reshape(1, g * p), lb_im.reshape(1, g * p)
    inv_den = lr * lr + li * li
    pp_re, pp_im = _complex_powers(lr, li, SSM_CHUNK)
    pn_re, pn_im = _complex_powers(lr / inv_den, -li / inv_den, SSM_CHUNK)
    return bbd, cbd, pn_re, pn_im, pp_re, pp_im, lr, li


def _attn_kernel(lam_ref, qt_ref, k_ref, vt_ref, g_ref, o_ref,
                 qz_ref, sa_ref, sb_ref, mxa_ref, mxb_ref, m_ref, acc_ref, *, tk, out_scale):
    i = pl.program_id(1)
    tq = qt_ref.shape[1]
    assert tq == tk
    qt = qt_ref[...]
    comp = lax.broadcasted_iota(jnp.int32, qt.shape, 0) < HEAD_DIM
    zero = jnp.zeros_like(qt)
    qz_ref[0] = jnp.where(comp, qt, zero)
    qz_ref[1] = jnp.where(comp, zero, qt)
    m_ref[...] = jnp.full_like(m_ref, NEG_INF)
    acc_ref[...] = jnp.zeros_like(acc_ref)

    def scores(j, s_ref, mx_ref, masked):
        start = pl.multiple_of(j * tk, tk)
        kb = k_ref[pl.ds(start, tk), :]
        for c in range(2):
            s = jnp.dot(kb, qz_ref[c], preferred_element_type=F32)
            if masked:
                key = lax.broadcasted_iota(jnp.int32, s.shape, 0)
                qry = lax.broadcasted_iota(jnp.int32, s.shape, 1)
                s = jnp.where(key <= qry, s, NEG_INF)
            s_ref[c] = s
            mx_ref[c] = jnp.max(s, axis=0, keepdims=True)

    def update(j, s_ref, mx_ref):
        start = pl.multiple_of(j * tk, tk)
        vtb = vt_ref[:, pl.ds(start, tk)]
        for c in range(2):
            m_prev = m_ref[c]
            m_new = jnp.maximum(m_prev, mx_ref[c])
            alpha = jnp.exp2(m_prev - m_new)
            p = jnp.exp2(s_ref[c] - m_new).astype(BF16)
            acc_ref[c] = alpha * acc_ref[c] + jnp.dot(vtb, p, preferred_element_type=F32)
            m_ref[c] = m_new

    @pl.when(i == 0)
    def _():
        scores(0, sa_ref, mxa_ref, True)
        update(0, sa_ref, mxa_ref)

    @pl.when(i > 0)
    def _():
        scores(0, sa_ref, mxa_ref, False)

    def pair(jj, carry):
        a = 2 * jj
        scores(a + 1, sb_ref, mxb_ref, False)
        update(a, sa_ref, mxa_ref)
        scores(a + 2, sa_ref, mxa_ref, False)
        update(a + 1, sb_ref, mxb_ref)
        return carry

    n_pairs = lax.shift_right_logical(jnp.maximum(i - 1, 0), 1)
    lax.fori_loop(0, n_pairs, pair, 0)
    first = 2 * n_pairs

    @pl.when(i % 2 == 1)
    def _():
        scores(i, sb_ref, mxb_ref, True)
        update(first, sa_ref, mxa_ref)
        update(i, sb_ref, mxb_ref)

    @pl.when((i % 2 == 0) & (i > 0))
    def _():
        scores(first + 1, sb_ref, mxb_ref, False)
        update(first, sa_ref, mxa_ref)
        scores(i, sa_ref, mxa_ref, True)
        update(first + 1, sb_ref, mxb_ref)
        update(i, sa_ref, mxa_ref)

    hw = 2 * HEAD_DIM
    o1 = acc_ref[0, :hw, :] / acc_ref[0, hw:hw + 1, :]
    o2 = acc_ref[1, :hw, :] / acc_ref[1, hw:hw + 1, :]
    ot = o1 - lam_ref[0, 0] * o2
    ms = jnp.mean(ot * ot, axis=0, keepdims=True)
    o = (ot * lax.rsqrt(ms + RMS_EPS)).T
    o_ref[...] = (o * g_ref[...] * out_scale).astype(o_ref.dtype)


def _attention(qt, k, vt, lam, subln_g, lam_init):
    n = k.shape[0]
    tq = min(ATTN_Q, n)
    tk = min(ATTN_K, n)
    hw = 2 * HEAD_DIM
    kernel = functools.partial(_attn_kernel, tk=tk, out_scale=1.0 - lam_init)
    return pl.pallas_call(
        kernel,
        grid=(N_HEADS, n // tq),
        in_specs=[
            pl.BlockSpec(memory_space=pltpu.SMEM),
            pl.BlockSpec((hw, tq), lambda h, i: (h, i)),
            pl.BlockSpec((n, hw), lambda h, i: (0, h)),
            pl.BlockSpec((VT_ROWS, n), lambda h, i: (h, 0)),
            pl.BlockSpec((1, hw), lambda h, i: (0, 0)),
        ],
        out_specs=pl.BlockSpec((tq, hw), lambda h, i: (i, h)),
        out_shape=jax.ShapeDtypeStruct((n, V_WIDTH), BF16),
        scratch_shapes=[
            pltpu.VMEM((2, hw, tq), BF16),
            pltpu.VMEM((2, tk, tq), F32), pltpu.VMEM((2, tk, tq), F32),
            pltpu.VMEM((2, 1, tq), F32), pltpu.VMEM((2, 1, tq), F32),
            pltpu.VMEM((2, 1, tq), F32),
            pltpu.VMEM((2, VT_ROWS, tq), F32),
        ],
        compiler_params=_params("parallel", "arbitrary"),
        name="diff_attn",
    )(lam, qt, k, vt, subln_g)


def _mix_kernel(x_ref, ys_ref, at_ref, sg_ref, wglu_ref, wbs_ref, wba_ref, wout_ref,
                g_ref, b_ref, o_ref):
    d = x_ref.shape[1]
    y = ys_ref[...]
    z = 0.5 * y * (1.0 + lax.erf(y * (2.0 ** -0.5)))
    gate = _sigmoid(jnp.dot(z.astype(BF16), wglu_ref[...], preferred_element_type=F32))
    glu = (z * gate).astype(BF16)
    ya = jnp.dot(glu, wbs_ref[...], preferred_element_type=F32)
    yb = jnp.dot(at_ref[...], wba_ref[...], preferred_element_type=F32)
    sg = sg_ref[...].astype(F32)
    mix = (sg[:, :d] * ya + sg[:, d:] * yb).astype(BF16)
    out = ALPHA * x_ref[...] + jnp.dot(mix, wout_ref[...], preferred_element_type=F32)
    o_ref[...] = _layer_norm(out, g_ref[...], b_ref[...])


def _resident(shape):
    return pl.BlockSpec(shape, lambda i: (0,) * len(shape), pipeline_mode=pl.Buffered(1))


def _resident_layer(w, layer):
    return _layer_block(layer, w.shape[1:], lambda i: (0, 0), pipeline_mode=pl.Buffered(1))


def _mix(x, ys, attn, sg, wglu, wbs, wba, wout, g, b, layer):
    n, d = x.shape
    tm = min(MIX_ROWS, n)
    rows = lambda w: pl.BlockSpec((tm, w), lambda i: (i, 0))
    return pl.pallas_call(
        _mix_kernel,
        grid=(n // tm,),
        in_specs=[
            rows(d), rows(ys.shape[1]), rows(attn.shape[1]), rows(sg.shape[1]),
            _resident_layer(wglu, layer), _resident_layer(wbs, layer), _resident_layer(wba, layer),
            _resident_layer(wout, layer),
            _resident((1, d)), _resident((1, d)),
        ],
        out_specs=rows(d),
        out_shape=jax.ShapeDtypeStruct((n, d), F32),
        compiler_params=_params("parallel"),
        name="mixer_out",
    )(x, ys, attn, sg, wglu, wbs, wba, wout, g, b)


def _ple_kernel(x_ref, p_ref, wg_ref, wp_ref, g_ref, b_ref, o_ref):
    x = x_ref[...]
    gate = _sigmoid(jnp.dot(x.astype(BF16), wg_ref[...], preferred_element_type=F32))
    emb = jnp.dot(p_ref[...].astype(BF16), wp_ref[...], preferred_element_type=F32)
    o_ref[...] = _layer_norm(ALPHA * x + gate * emb, g_ref[...], b_ref[...])


def _ple(x, p, wg, wp, g, b, layer):
    n, d = x.shape
    tm = min(PLE_ROWS, n)
    rows = lambda w: pl.BlockSpec((tm, w), lambda i: (i, 0))
    return pl.pallas_call(
        _ple_kernel,
        grid=(n // tm,),
        in_specs=[rows(d), pl.BlockSpec((None, None, tm, p.shape[3]), lambda i: (layer, 0, i, 0)),
                  _resident_layer(wg, layer), _resident_layer(wp, layer),
                  _resident((1, d)), _resident((1, d))],
        out_specs=rows(d),
        out_shape=jax.ShapeDtypeStruct((n, d), F32),
        compiler_params=_params("parallel"),
        name="ple_ln",
    )(x, p, wg, wp, g, b)


def _rope_tables(positions):
    half = ROT_DIM // 2
    inv_freq = ROPE_THETA ** (-jnp.arange(0, ROT_DIM, 2, dtype=F32) / ROT_DIM)
    ang = positions.astype(F32)[:, None] * inv_freq
    cos, sin = jnp.cos(ang), jnp.sin(ang)
    n = positions.shape[0]
    pad = jnp.zeros((n, HEAD_DIM - ROT_DIM), F32)
    zeros = jnp.zeros((n, half), F32)
    cos_t = jnp.concatenate([cos, cos, pad + 1.0], axis=1)
    sin_lo = jnp.concatenate([-sin, zeros, pad], axis=1)
    sin_hi = jnp.concatenate([zeros, sin, pad], axis=1)
    reps = LANES // HEAD_DIM
    return tuple(jnp.tile(t, (1, reps)) for t in (cos_t, sin_lo, sin_hi))


def kernel(x, p, positions, ffn1_w_gate, ffn1_w_up, ffn1_w_down, ln1_g, ln1_b, w_in, ssm_a_re, ssm_a_im, ssm_b_re, ssm_b_im, ssm_c_re, ssm_c_im, ssm_log_dt, ssm_d, ssm_w_glu, w_branch_ssm, lambda_q1, lambda_k1, lambda_q2, lambda_k2, attn_subln_g, w_branch_attn, w_out, ln2_g, ln2_b, ffn2_w_gate, ffn2_w_up, ffn2_w_down, ln3_g, ln3_b, ple_w_gate, ple_w_proj, ln4_g, ln4_b):
    bsz, length, d = x.shape
    assert bsz == 1
    xs = x.reshape(length, d)
    cos, slo, shi = _rope_tables(positions[0])
    tri = jnp.kron(jnp.eye(SSM_ROWS // SSM_CHUNK, dtype=F32),
                   jnp.tril(jnp.ones((SSM_CHUNK, SSM_CHUNK), F32))).astype(BF16)
    row = lambda a: a.reshape(1, -1)
    w_ffn1 = (_to_bf16(ffn1_w_gate, col_tile=FFN_COLS), _to_bf16(ffn1_w_up, col_tile=FFN_COLS),
              _to_bf16(ffn1_w_down, 0.5))
    w_ffn2 = (_to_bf16(ffn2_w_gate, col_tile=FFN_COLS), _to_bf16(ffn2_w_up, col_tile=FFN_COLS),
              _to_bf16(ffn2_w_down, 0.5))
    w_inb = _to_bf16(w_in, col_tile=PROJ_COLS)
    w_mix = (_to_bf16(ssm_w_glu), _to_bf16(w_branch_ssm), _to_bf16(w_branch_attn), _to_bf16(w_out))
    w_ple = (_to_bf16(ple_w_gate), _to_bf16(ple_w_proj))
    tables = jax.vmap(_ssm_tables)(ssm_a_re, ssm_a_im, ssm_b_re, ssm_b_im, ssm_c_re, ssm_c_im, ssm_log_dt)
    for i in range(DEPTH):
        lam_init = 0.8 - 0.6 * math.exp(-0.3 * i)
        lam = (jnp.exp(jnp.sum(lambda_q1[i] * lambda_k1[i])) - jnp.exp(jnp.sum(lambda_q2[i] * lambda_k2[i]))
               + lam_init).reshape(1, 1)
        xs, xb = _ffn_ln(xs, *w_ffn1, row(ln1_g[i]), row(ln1_b[i]), layer=i, emit_bf16=True)
        u, qt, k, vt, sg = _inproj(xb, w_inb, cos, slo, shi, layer=i)
        ys = _ssm(u, tables, row(ssm_d[i]), tri, layer=i)
        attn = _attention(qt, k, vt, lam, row(attn_subln_g[i]), lam_init)
        xs = _mix(xs, ys, attn, sg, *w_mix, row(ln2_g[i]), row(ln2_b[i]), layer=i)
        xs = _ffn_ln(xs, *w_ffn2, row(ln3_g[i]), row(ln3_b[i]), layer=i, emit_bf16=False)
        xs = _ple(xs, p, *w_ple, row(ln4_g[i]), row(ln4_b[i]), layer=i)
    return xs.reshape(bsz, length, d)
```

```python
import functools
import math

import jax
import jax.numpy as jnp
from jax import lax
from jax.experimental import pallas as pl
from jax.experimental.pallas import tpu as pltpu

F32 = jnp.float32
BF16 = jnp.bfloat16

DEPTH = 2
SSM_WIDTH = 1024
SSM_GROUP = 16
SSM_GROUPS = SSM_WIDTH // SSM_GROUP
SSM_STATE = 64
N_HEADS = 8
HEAD_DIM = 64
QK_WIDTH = N_HEADS * 2 * HEAD_DIM
V_WIDTH = N_HEADS * 2 * HEAD_DIM
ROT_DIM = HEAD_DIM // 4
ROPE_THETA = 500000.0
LN_EPS = 1e-5
RMS_EPS = 1e-5
NEG_INF = -1e30
ALPHA = (2 * DEPTH) ** 0.25
Q_SCALE = HEAD_DIM ** -0.5 * math.log2(math.e)

LANES = 128
BF16_ROWS = 16
VT_ROWS = 2 * HEAD_DIM + BF16_ROWS
VMEM_LIMIT_BYTES = 56 * 1024 * 1024
CAST_BLOCK_BYTES = 6 * 1024 * 1024

FFN_ROWS = 512
FFN_COLS = 512
PROJ_ROWS = 1024
PROJ_COLS = 1024
PROJ_CHUNK = 256
SSM_CHUNK = 128
SSM_ROWS = 256
SSM_SLABS = 4
ATTN_Q = 1024
ATTN_K = 1024
MIX_ROWS = 512
PLE_ROWS = 512


def _sigmoid(x):
    return 1.0 / (1.0 + jnp.exp(-x))


def _layer_norm(y, g, b):
    mu = jnp.mean(y, axis=-1, keepdims=True)
    yc = y - mu
    var = jnp.mean(yc * yc, axis=-1, keepdims=True)
    return yc * lax.rsqrt(var + LN_EPS) * g + b


def _params(*semantics):
    return pltpu.CompilerParams(dimension_semantics=semantics, vmem_limit_bytes=VMEM_LIMIT_BYTES)


def _cast_kernel(w_ref, o_ref, *, scale):
    w = w_ref[...]
    if scale != 1.0:
        w = w * scale
    if len(o_ref.shape) == 2:
        o_ref[...] = w.astype(BF16)
    else:
        tc = o_ref.shape[2]
        for t in range(o_ref.shape[0]):
            o_ref[t] = w[:, t * tc:(t + 1) * tc].astype(BF16)


def _to_bf16(w, scale=1.0, col_tile=None):
    layers, r, c = w.shape
    fits = [br for br in range(BF16_ROWS, r + 1, BF16_ROWS) if r % br == 0 and br * c * 4 <= CAST_BLOCK_BYTES]
    br = max(fits) if fits else r
    in_spec = pl.BlockSpec((None, br, c), lambda l, i: (l, i, 0))
    if col_tile is None:
        out_spec, out_shape = in_spec, w.shape
    else:
        tiles = c // col_tile
        out_spec = pl.BlockSpec((None, tiles, br, col_tile), lambda l, i: (l, 0, i, 0))
        out_shape = (layers, tiles, r, col_tile)
    return pl.pallas_call(
        functools.partial(_cast_kernel, scale=scale),
        grid=(layers, r // br),
        in_specs=[in_spec],
        out_specs=out_spec,
        out_shape=jax.ShapeDtypeStruct(out_shape, BF16),
        compiler_params=_params("parallel", "parallel"),
        name="cast_bf16",
    )(w)


def _layer_block(layer, shape, index, **kwargs):
    return pl.BlockSpec((None,) + shape, lambda *g: (layer,) + tuple(index(*g)), **kwargs)


def _ffn_ln_kernel(x_ref, wg_ref, wu_ref, wd_ref, g_ref, b_ref, *rest, emit_bf16):
    if emit_bf16:
        o_ref, ob_ref, xb_ref, acc_ref = rest
    else:
        o_ref, xb_ref, acc_ref = rest
    j = pl.program_id(1)

    @pl.when(j == 0)
    def _():
        xb_ref[...] = x_ref[...].astype(BF16)
        acc_ref[...] = jnp.zeros_like(acc_ref)

    xb = xb_ref[...]
    hg = jnp.dot(xb, wg_ref[...], preferred_element_type=F32)
    hu = jnp.dot(xb, wu_ref[...], preferred_element_type=F32)
    act = (hg * _sigmoid(hg) * hu).astype(BF16)
    acc_ref[...] += jnp.dot(act, wd_ref[...], preferred_element_type=F32)

    @pl.when(j == pl.num_programs(1) - 1)
    def _():
        y = _layer_norm(ALPHA * x_ref[...] + acc_ref[...], g_ref[...], b_ref[...])
        o_ref[...] = y
        if emit_bf16:
            ob_ref[...] = y.astype(BF16)


def _ffn_ln(x, wg, wu, wd, g, b, layer, emit_bf16):
    n, d = x.shape
    n_tiles, tf = wg.shape[1], wg.shape[3]
    tm = min(FFN_ROWS, n)
    rows = pl.BlockSpec((tm, d), lambda i, j: (i, 0))
    return pl.pallas_call(
        functools.partial(_ffn_ln_kernel, emit_bf16=emit_bf16),
        grid=(n // tm, n_tiles),
        in_specs=[
            rows,
            _layer_block(layer, (None, d, tf), lambda i, j: (j, 0, 0)),
            _layer_block(layer, (None, d, tf), lambda i, j: (j, 0, 0)),
            _layer_block(layer, (tf, d), lambda i, j: (j, 0)),
            pl.BlockSpec((1, d), lambda i, j: (0, 0)),
            pl.BlockSpec((1, d), lambda i, j: (0, 0)),
        ],
        out_specs=[rows, rows] if emit_bf16 else rows,
        out_shape=([jax.ShapeDtypeStruct((n, d), F32), jax.ShapeDtypeStruct((n, d), BF16)] if emit_bf16
                   else jax.ShapeDtypeStruct((n, d), F32)),
        scratch_shapes=[pltpu.VMEM((tm, d), BF16), pltpu.VMEM((tm, d), F32)],
        compiler_params=_params("parallel", "arbitrary"),
        name="ffn_ln",
    )(x, wg, wu, wd, g, b)


def _rotary(h, cos, sin_lo, sin_hi):
    half = ROT_DIM // 2
    return (h * cos + pltpu.roll(h, half, 1) * sin_hi
            + pltpu.roll(h, LANES - half, 1) * sin_lo)


def _inproj_kernel(xb_ref, w_ref, cos_ref, slo_ref, shi_ref, u_ref, qt_ref, k_ref, vt_ref, g_ref):
    j = pl.program_id(1)

    def for_chunks(store):
        xb = xb_ref[...]
        for c in range(w_ref.shape[1] // PROJ_CHUNK):
            cols = slice(c * PROJ_CHUNK, (c + 1) * PROJ_CHUNK)
            h = jnp.dot(xb, w_ref[:, cols], preferred_element_type=F32)
            for s in range(PROJ_CHUNK // LANES):
                lo = c * PROJ_CHUNK + s * LANES
                store(slice(lo, lo + LANES), h[:, s * LANES:(s + 1) * LANES])

    def rope(h):
        return _rotary(h, cos_ref[...], slo_ref[...], shi_ref[...])

    @pl.when(j == 0)
    def _():
        def store(cols, h):
            u_ref[:, cols] = h
        for_chunks(store)

    @pl.when(j == 1)
    def _():
        def store(cols, h):
            qt_ref[cols, :] = (rope(h) * Q_SCALE).T.astype(BF16)
        for_chunks(store)

    @pl.when(j == 2)
    def _():
        def store(cols, h):
            k_ref[:, cols] = rope(h).astype(BF16)
        for_chunks(store)

    @pl.when(j == 3)
    def _():
        def store(cols, h):
            base = cols.start // LANES * VT_ROWS
            vt_ref[base:base + LANES, :] = h.T.astype(BF16)
            vt_ref[base + LANES:base + VT_ROWS, :] = jnp.ones((VT_ROWS - LANES, h.shape[0]), BF16)
        for_chunks(store)

    @pl.when(j >= 4)
    def _():
        def store(cols, h):
            g_ref[:, cols] = _sigmoid(h).astype(BF16)
        for_chunks(store)


def _inproj(x, w, cos, slo, shi, layer):
    n, d = x.shape
    tm = min(PROJ_ROWS, n)
    tn = w.shape[3]
    width = w.shape[1] * tn
    n_gate = (width - SSM_WIDTH - 2 * QK_WIDTH - V_WIDTH) // tn
    assert SSM_WIDTH == tn and QK_WIDTH == tn and V_WIDTH == tn
    tab = pl.BlockSpec((tm, LANES), lambda i, j: (i, 0))
    return pl.pallas_call(
        _inproj_kernel,
        grid=(n // tm, width // tn),
        in_specs=[
            pl.BlockSpec((tm, d), lambda i, j: (i, 0)),
            _layer_block(layer, (None, d, tn), lambda i, j: (j, 0, 0)),
            tab, tab, tab,
        ],
        out_specs=[
            pl.BlockSpec((tm, tn), lambda i, j: (i, 0)),
            pl.BlockSpec((tn, tm), lambda i, j: (0, i)),
            pl.BlockSpec((tm, tn), lambda i, j: (i, 0)),
            pl.BlockSpec((N_HEADS * VT_ROWS, tm), lambda i, j: (0, i)),
            pl.BlockSpec((tm, tn), lambda i, j: (i, jnp.clip(j - 4, 0, n_gate - 1))),
        ],
        out_shape=[
            jax.ShapeDtypeStruct((n, SSM_WIDTH), F32),
            jax.ShapeDtypeStruct((QK_WIDTH, n), BF16),
            jax.ShapeDtypeStruct((n, QK_WIDTH), BF16),
            jax.ShapeDtypeStruct((N_HEADS * VT_ROWS, n), BF16),
            jax.ShapeDtypeStruct((n, n_gate * tn), BF16),
        ],
        compiler_params=_params("parallel", "arbitrary"),
        name="inproj",
    )(x, w, cos, slo, shi)


def _ssm_kernel(u_ref, bbd_ref, cbd_ref, pn_re_ref, pn_im_ref, pp_re_ref, pp_im_ref,
                lb_re_ref, lb_im_ref, d_ref, tri_ref, y_ref, car_re_ref, car_im_ref):
    @pl.when(pl.program_id(0) == 0)
    def _():
        car_re_ref[...] = jnp.zeros_like(car_re_ref)
        car_im_ref[...] = jnp.zeros_like(car_im_ref)

    t_len = pn_re_ref.shape[0]
    n_sub = u_ref.shape[0] // t_len
    slab_in = u_ref.shape[1] // SSM_SLABS
    slab_st = pn_re_ref.shape[1] // SSM_SLABS
    u = u_ref[...]
    ub = u.astype(BF16)
    tri = tri_ref[...]
    for s in range(SSM_SLABS):
        cin = slice(s * slab_in, (s + 1) * slab_in)
        cst = slice(s * slab_st, (s + 1) * slab_st)
        bu = jnp.dot(ub[:, cin], bbd_ref[s], preferred_element_type=F32)
        nr, ni = pn_re_ref[:, cst], pn_im_ref[:, cst]
        z = []
        for k in range(n_sub):
            br = bu[k * t_len:(k + 1) * t_len, :slab_st]
            bi = bu[k * t_len:(k + 1) * t_len, slab_st:]
            z.append(jnp.concatenate([br * nr - bi * ni, br * ni + bi * nr], axis=1).astype(BF16))
        cs = jnp.dot(tri, jnp.concatenate(z, axis=0), preferred_element_type=F32)
        lr, li = lb_re_ref[:, cst], lb_im_ref[:, cst]
        pr, pi = pp_re_ref[:, cst], pp_im_ref[:, cst]
        kr, ki = car_re_ref[:, cst], car_im_ref[:, cst]
        states = []
        for k in range(n_sub):
            cr = cs[k * t_len:(k + 1) * t_len, :slab_st] + (lr * kr - li * ki)
            ci = cs[k * t_len:(k + 1) * t_len, slab_st:] + (lr * ki + li * kr)
            sr = cr * pr - ci * pi
            si = cr * pi + ci * pr
            kr, ki = sr[t_len - 1:t_len, :], si[t_len - 1:t_len, :]
            states.append(jnp.concatenate([sr, si], axis=1).astype(BF16))
        car_re_ref[:, cst] = kr
        car_im_ref[:, cst] = ki
        y = jnp.dot(jnp.concatenate(states, axis=0), cbd_ref[s], preferred_element_type=F32)
        y_ref[:, cin] = y + d_ref[:, cin] * u[:, cin]


def _ssm(u, tables, d_skip, tri, layer):
    n, w = u.shape
    rows = tri.shape[0]
    ns = tables[2].shape[2]
    const2 = lambda shape: pl.BlockSpec(shape, lambda i: (0, 0))
    table = lambda a: _layer_block(layer, a.shape[1:], lambda i: (0,) * (a.ndim - 1))
    return pl.pallas_call(
        _ssm_kernel,
        grid=(n // rows,),
        in_specs=[
            pl.BlockSpec((rows, w), lambda i: (i, 0)),
            *[table(a) for a in tables],
            const2((1, w)), const2((rows, rows)),
        ],
        out_specs=pl.BlockSpec((rows, w), lambda i: (i, 0)),
        out_shape=jax.ShapeDtypeStruct((n, w), F32),
        scratch_shapes=[pltpu.VMEM((1, ns), F32), pltpu.VMEM((1, ns), F32)],
        compiler_params=_params("arbitrary"),
        name="ssm_scan",
    )(u, *tables, d_skip, tri)


def _complex_powers(lr, li, count):
    pr, pi = jnp.ones_like(lr), jnp.zeros_like(li)
    cr, ci = lr, li
    rows = 1
    while rows < count:
        pr, pi = (jnp.concatenate([pr, pr * cr - pi * ci], axis=0),
                  jnp.concatenate([pi, pr * ci + pi * cr], axis=0))
        cr, ci = cr * cr - ci * ci, 2.0 * cr * ci
        rows *= 2
    return pr[:count], pi[:count]


def _ssm_tables(a_re, a_im, b_re, b_im, c_re, c_im, log_dt):
    g, p = a_re.shape
    h = b_re.shape[2]
    dt = jnp.exp(log_dt)[:, None]
    mag = jnp.exp(a_re * dt)
    lb_re = mag * jnp.cos(a_im * dt)
    lb_im = mag * jnp.sin(a_im * dt)
    nr, ni = lb_re - 1.0, lb_im
    den = a_re * a_re + a_im * a_im
    coef_re = (nr * a_re + ni * a_im) / den
    coef_im = (ni * a_re - nr * a_im) / den
    bb_re = coef_re[:, :, None] * b_re - coef_im[:, :, None] * b_im
    bb_im = coef_re[:, :, None] * b_im + coef_im[:, :, None] * b_re
    gs = g // SSM_SLABS
    eye = jnp.eye(gs, dtype=F32)

    def in_blocks(bb):
        t = bb.reshape(SSM_SLABS, gs, p, h).transpose(0, 1, 3, 2)
        return jnp.einsum("ab,sbhp->sahbp", eye, t).reshape(SSM_SLABS, gs * h, gs * p)

    def out_blocks(cc):
        t = cc.reshape(SSM_SLABS, gs, h, p).transpose(0, 1, 3, 2)
        return jnp.einsum("ab,sbph->sbpah", eye, t).reshape(SSM_SLABS, gs * p, gs * h)

    bbd = jnp.concatenate([in_blocks(bb_re), in_blocks(bb_im)], axis=2).astype(BF16)
    cbd = jnp.concatenate([out_blocks(c_re), -out_blocks(c_im)], axis=1).astype(BF16)
    lr, li = lb_re.reshape(1, g * p), lb_im.reshape(1, g * p)
    inv_den = lr * lr + li * li
    pp_re, pp_im = _complex_powers(lr, li, SSM_CHUNK)
    pn_re, pn_im = _complex_powers(lr / inv_den, -li / inv_den, SSM_CHUNK)
    return bbd, cbd, pn_re, pn_im, pp_re, pp_im, lr, li


def _attn_kernel(lam_ref, qt_ref, k_ref, vt_ref, g_ref, o_ref,
                 qz_ref, sa_ref, sb_ref, mxa_ref, mxb_ref, m_ref, acc_ref, *, tk, out_scale):
    i = pl.program_id(1)
    tq = qt_ref.shape[1]
    assert tq == tk
    qt = qt_ref[...]
    comp = lax.broadcasted_iota(jnp.int32, qt.shape, 0) < HEAD_DIM
    zero = jnp.zeros_like(qt)
    qz_ref[0] = jnp.where(comp, qt, zero)
    qz_ref[1] = jnp.where(comp, zero, qt)
    m_ref[...] = jnp.full_like(m_ref, NEG_INF)
    acc_ref[...] = jnp.zeros_like(acc_ref)

    half = tk // 2

    def causal(s):
        key = lax.broadcasted_iota(jnp.int32, s.shape, 0)
        qry = lax.broadcasted_iota(jnp.int32, s.shape, 1)
        return jnp.where(key <= qry, s, NEG_INF)

    def scores(j, s_ref, mx_ref):
        start = pl.multiple_of(j * tk, tk)
        kb = k_ref[pl.ds(start, tk), :]
        for c in range(2):
            s = jnp.dot(kb, qz_ref[c], preferred_element_type=F32)
            s_ref[c] = s
            mx_ref[c] = jnp.max(s, axis=0, keepdims=True)

    def update(j, s_ref, mx_ref):
        start = pl.multiple_of(j * tk, tk)
        vtb = vt_ref[:, pl.ds(start, tk)]
        for c in range(2):
            m_prev = m_ref[c]
            m_new = jnp.maximum(m_prev, mx_ref[c])
            alpha = jnp.exp2(m_prev - m_new)
            p = jnp.exp2(s_ref[c] - m_new).astype(BF16)
            acc_ref[c] = alpha * acc_ref[c] + jnp.dot(vtb, p, preferred_element_type=F32)
            m_ref[c] = m_new

    def scores_diag(s_ref, mx_ref):
        start = pl.multiple_of(i * tk, tk)
        k_lo = k_ref[pl.ds(start, half), :]
        k_hi = k_ref[pl.ds(start + half, half), :]
        for c in range(2):
            s_lo = causal(jnp.dot(k_lo, qz_ref[c], preferred_element_type=F32))
            s_hi = causal(jnp.dot(k_hi, qz_ref[c, :, half:], preferred_element_type=F32))
            s_ref[c, :half, :] = s_lo
            s_ref[c, half:, half:] = s_hi
            mx_lo = jnp.max(s_lo, axis=0, keepdims=True)
            mx_hi = jnp.max(s_hi, axis=0, keepdims=True)
            mx_ref[c, :, :half] = mx_lo[:, :half]
            mx_ref[c, :, half:] = jnp.maximum(mx_lo[:, half:], mx_hi)

    def update_diag(s_ref, mx_ref):
        start = pl.multiple_of(i * tk, tk)
        vt_lo = vt_ref[:, pl.ds(start, half)]
        vt_hi = vt_ref[:, pl.ds(start + half, half)]
        for c in range(2):
            m_prev = m_ref[c]
            m_new = jnp.maximum(m_prev, mx_ref[c])
            alpha = jnp.exp2(m_prev - m_new)
            p_lo = jnp.exp2(s_ref[c, :half, :] - m_new).astype(BF16)
            p_hi = jnp.exp2(s_ref[c, half:, half:] - m_new[:, half:]).astype(BF16)
            from_lo = jnp.dot(vt_lo, p_lo, preferred_element_type=F32)
            from_hi = jnp.dot(vt_hi, p_hi, preferred_element_type=F32)
            acc_ref[c, :, :half] = alpha[:, :half] * acc_ref[c, :, :half] + from_lo[:, :half]
            acc_ref[c, :, half:] = alpha[:, half:] * acc_ref[c, :, half:] + (from_lo[:, half:] + from_hi)
            m_ref[c] = m_new

    @pl.when(i == 0)
    def _():
        scores_diag(sa_ref, mxa_ref)
        update_diag(sa_ref, mxa_ref)

    @pl.when(i > 0)
    def _():
        scores(0, sa_ref, mxa_ref)

    def pair(jj, carry):
        a = 2 * jj
        scores(a + 1, sb_ref, mxb_ref)
        update(a, sa_ref, mxa_ref)
        scores(a + 2, sa_ref, mxa_ref)
        update(a + 1, sb_ref, mxb_ref)
        return carry

    n_pairs = lax.shift_right_logical(jnp.maximum(i - 1, 0), 1)
    lax.fori_loop(0, n_pairs, pair, 0)
    first = 2 * n_pairs

    @pl.when(i % 2 == 1)
    def _():
        scores_diag(sb_ref, mxb_ref)
        update(first, sa_ref, mxa_ref)
        update_diag(sb_ref, mxb_ref)

    @pl.when((i % 2 == 0) & (i > 0))
    def _():
        scores(first + 1, sb_ref, mxb_ref)
        update(first, sa_ref, mxa_ref)
        scores_diag(sa_ref, mxa_ref)
        update(first + 1, sb_ref, mxb_ref)
        update_diag(sa_ref, mxa_ref)

    hw = 2 * HEAD_DIM
    o1 = acc_ref[0, :hw, :] / acc_ref[0, hw:hw + 1, :]
    o2 = acc_ref[1, :hw, :] / acc_ref[1, hw:hw + 1, :]
    ot = o1 - lam_ref[0, 0] * o2
    ms = jnp.mean(ot * ot, axis=0, keepdims=True)
    o = (ot * lax.rsqrt(ms + RMS_EPS)).T
    o_ref[...] = (o * g_ref[...] * out_scale).astype(o_ref.dtype)


def _attention(qt, k, vt, lam, subln_g, lam_init):
    n = k.shape[0]
    tq = min(ATTN_Q, n)
    tk = min(ATTN_K, n)
    hw = 2 * HEAD_DIM
    kernel = functools.partial(_attn_kernel, tk=tk, out_scale=1.0 - lam_init)
    return pl.pallas_call(
        kernel,
        grid=(N_HEADS, n // tq),
        in_specs=[
            pl.BlockSpec(memory_space=pltpu.SMEM),
            pl.BlockSpec((hw, tq), lambda h, i: (h, i)),
            pl.BlockSpec((n, hw), lambda h, i: (0, h)),
            pl.BlockSpec((VT_ROWS, n), lambda h, i: (h, 0)),
            pl.BlockSpec((1, hw), lambda h, i: (0, 0)),
        ],
        out_specs=pl.BlockSpec((tq, hw), lambda h, i: (i, h)),
        out_shape=jax.ShapeDtypeStruct((n, V_WIDTH), BF16),
        scratch_shapes=[
            pltpu.VMEM((2, hw, tq), BF16),
            pltpu.VMEM((2, tk, tq), F32), pltpu.VMEM((2, tk, tq), F32),
            pltpu.VMEM((2, 1, tq), F32), pltpu.VMEM((2, 1, tq), F32),
            pltpu.VMEM((2, 1, tq), F32),
            pltpu.VMEM((2, VT_ROWS, tq), F32),
        ],
        compiler_params=_params("parallel", "arbitrary"),
        name="diff_attn",
    )(lam, qt, k, vt, subln_g)


def _mix_kernel(x_ref, ys_ref, at_ref, sg_ref, wglu_ref, wbs_ref, wba_ref, wout_ref,
                g_ref, b_ref, o_ref):
    d = x_ref.shape[1]
    y = ys_ref[...]
    z = 0.5 * y * (1.0 + lax.erf(y * (2.0 ** -0.5)))
    gate = _sigmoid(jnp.dot(z.astype(BF16), wglu_ref[...], preferred_element_type=F32))
    glu = (z * gate).astype(BF16)
    ya = jnp.dot(glu, wbs_ref[...], preferred_element_type=F32)
    yb = jnp.dot(at_ref[...], wba_ref[...], preferred_element_type=F32)
    sg = sg_ref[...].astype(F32)
    mix = (sg[:, :d] * ya + sg[:, d:] * yb).astype(BF16)
    out = ALPHA * x_ref[...] + jnp.dot(mix, wout_ref[...], preferred_element_type=F32)
    o_ref[...] = _layer_norm(out, g_ref[...], b_ref[...])


def _resident(shape):
    return pl.BlockSpec(shape, lambda i: (0,) * len(shape), pipeline_mode=pl.Buffered(1))


def _resident_layer(w, layer):
    return _layer_block(layer, w.shape[1:], lambda i: (0, 0), pipeline_mode=pl.Buffered(1))


def _mix(x, ys, attn, sg, wglu, wbs, wba, wout, g, b, layer):
    n, d = x.shape
    tm = min(MIX_ROWS, n)
    rows = lambda w: pl.BlockSpec((tm, w), lambda i: (i, 0))
    return pl.pallas_call(
        _mix_kernel,
        grid=(n // tm,),
        in_specs=[
            rows(d), rows(ys.shape[1]), rows(attn.shape[1]), rows(sg.shape[1]),
            _resident_layer(wglu, layer), _resident_layer(wbs, layer), _resident_layer(wba, layer),
            _resident_layer(wout, layer),
            _resident((1, d)), _resident((1, d)),
        ],
        out_specs=rows(d),
        out_shape=jax.ShapeDtypeStruct((n, d), F32),
        compiler_params=_params("parallel"),
        name="mixer_out",
    )(x, ys, attn, sg, wglu, wbs, wba, wout, g, b)


def _ple_kernel(x_ref, p_ref, wg_ref, wp_ref, g_ref, b_ref, o_ref):
    x = x_ref[...]
    gate = _sigmoid(jnp.dot(x.astype(BF16), wg_ref[...], preferred_element_type=F32))
    emb = jnp.dot(p_ref[...].astype(BF16), wp_ref[...], preferred_element_type=F32)
    o_ref[...] = _layer_norm(ALPHA * x + gate * emb, g_ref[...], b_ref[...])


def _ple(x, p, wg, wp, g, b, layer):
    n, d = x.shape
    tm = min(PLE_ROWS, n)
    rows = lambda w: pl.BlockSpec((tm, w), lambda i: (i, 0))
    return pl.pallas_call(
        _ple_kernel,
        grid=(n // tm,),
        in_specs=[rows(d), pl.BlockSpec((None, None, tm, p.shape[3]), lambda i: (layer, 0, i, 0)),
                  _resident_layer(wg, layer), _resident_layer(wp, layer),
                  _resident((1, d)), _resident((1, d))],
        out_specs=rows(d),
        out_shape=jax.ShapeDtypeStruct((n, d), F32),
        compiler_params=_params("parallel"),
        name="ple_ln",
    )(x, p, wg, wp, g, b)


def _rope_tables(positions):
    half = ROT_DIM // 2
    inv_freq = ROPE_THETA ** (-jnp.arange(0, ROT_DIM, 2, dtype=F32) / ROT_DIM)
    ang = positions.astype(F32)[:, None] * inv_freq
    cos, sin = jnp.cos(ang), jnp.sin(ang)
    n = positions.shape[0]
    pad = jnp.zeros((n, HEAD_DIM - ROT_DIM), F32)
    zeros = jnp.zeros((n, half), F32)
    cos_t = jnp.concatenate([cos, cos, pad + 1.0], axis=1)
    sin_lo = jnp.concatenate([-sin, zeros, pad], axis=1)
    sin_hi = jnp.concatenate([zeros, sin, pad], axis=1)
    reps = LANES // HEAD_DIM
    return tuple(jnp.tile(t, (1, reps)) for t in (cos_t, sin_lo, sin_hi))


def kernel(x, p, positions, ffn1_w_gate, ffn1_w_up, ffn1_w_down, ln1_g, ln1_b, w_in, ssm_a_re, ssm_a_im, ssm_b_re, ssm_b_im, ssm_c_re, ssm_c_im, ssm_log_dt, ssm_d, ssm_w_glu, w_branch_ssm, lambda_q1, lambda_k1, lambda_q2, lambda_k2, attn_subln_g, w_branch_attn, w_out, ln2_g, ln2_b, ffn2_w_gate, ffn2_w_up, ffn2_w_down, ln3_g, ln3_b, ple_w_gate, ple_w_proj, ln4_g, ln4_b):
    bsz, length, d = x.shape
    assert bsz == 1
    xs = x.reshape(length, d)
    cos, slo, shi = _rope_tables(positions[0])
    tri = jnp.kron(jnp.eye(SSM_ROWS // SSM_CHUNK, dtype=F32),
                   jnp.tril(jnp.ones((SSM_CHUNK, SSM_CHUNK), F32))).astype(BF16)
    row = lambda a: a.reshape(1, -1)
    w_ffn1 = (_to_bf16(ffn1_w_gate, col_tile=FFN_COLS), _to_bf16(ffn1_w_up, col_tile=FFN_COLS),
              _to_bf16(ffn1_w_down, 0.5))
    w_ffn2 = (_to_bf16(ffn2_w_gate, col_tile=FFN_COLS), _to_bf16(ffn2_w_up, col_tile=FFN_COLS),
              _to_bf16(ffn2_w_down, 0.5))
    w_inb = _to_bf16(w_in, col_tile=PROJ_COLS)
    w_mix = (_to_bf16(ssm_w_glu), _to_bf16(w_branch_ssm), _to_bf16(w_branch_attn), _to_bf16(w_out))
    w_ple = (_to_bf16(ple_w_gate), _to_bf16(ple_w_proj))
    tables = jax.vmap(_ssm_tables)(ssm_a_re, ssm_a_im, ssm_b_re, ssm_b_im, ssm_c_re, ssm_c_im, ssm_log_dt)
    for i in range(DEPTH):
        lam_init = 0.8 - 0.6 * math.exp(-0.3 * i)
        lam = (jnp.exp(jnp.sum(lambda_q1[i] * lambda_k1[i])) - jnp.exp(jnp.sum(lambda_q2[i] * lambda_k2[i]))
               + lam_init).reshape(1, 1)
        xs, xb = _ffn_ln(xs, *w_ffn1, row(ln1_g[i]), row(ln1_b[i]), layer=i, emit_bf16=True)
        u, qt, k, vt, sg = _inproj(xb, w_inb, cos, slo, shi, layer=i)
        ys = _ssm(u, tables, row(ssm_d[i]), tri, layer=i)
        attn = _attention(qt, k, vt, lam, row(attn_subln_g[i]), lam_init)
        xs = _mix(xs, ys, attn, sg, *w_mix, row(ln2_g[i]), row(ln2_b[i]), layer=i)
        xs = _ffn_ln(xs, *w_ffn2, row(ln3_g[i]), row(ln3_b[i]), layer=i, emit_bf16=False)
        xs = _ple(xs, p, *w_ple, row(ln4_g[i]), row(ln4_b[i]), layer=i)
    return xs.reshape(bsz, length, d)
```

```python
import functools
import math

import jax
import jax.numpy as jnp
from jax import lax
from jax.experimental import pallas as pl
from jax.experimental.pallas import tpu as pltpu

F32 = jnp.float32
BF16 = jnp.bfloat16

DEPTH = 2
SSM_WIDTH = 1024
N_HEADS = 8
HEAD_DIM = 64
QK_WIDTH = N_HEADS * 2 * HEAD_DIM
V_WIDTH = N_HEADS * 2 * HEAD_DIM
ROT_DIM = HEAD_DIM // 4
ROPE_THETA = 500000.0
LN_EPS = 1e-5
RMS_EPS = 1e-5
NEG_INF = -1e30
ALPHA = (2 * DEPTH) ** 0.25
Q_SCALE = HEAD_DIM ** -0.5 * math.log2(math.e)

LANES = 128
BF16_ROWS = 16
VMEM_BYTES = 64 * 1024 * 1024
VMEM_LIMIT_BYTES = VMEM_BYTES - 8 * 1024 * 1024
VT_ROWS = 2 * HEAD_DIM + BF16_ROWS
CAST_BLOCK_BYTES = 6 * 1024 * 1024

FFN_ROWS = 512
FFN_COLS = 512
PROJ_ROWS = 1024
PROJ_COLS = 1024
PROJ_CHUNK = 256
SSM_CHUNK = 128
SSM_ROWS = 256
SSM_SLABS = 4
ATTN_Q = 1024
ATTN_K = 1024
MIX_ROWS = 512
PLE_ROWS = 512


def _sigmoid(x):
    return 1.0 / (1.0 + jnp.exp(-x))


def _layer_norm(y, g, b):
    mu = jnp.mean(y, axis=-1, keepdims=True)
    yc = y - mu
    var = jnp.mean(yc * yc, axis=-1, keepdims=True)
    return yc * lax.rsqrt(var + LN_EPS) * g + b


def _params(*semantics):
    return pltpu.CompilerParams(dimension_semantics=semantics, vmem_limit_bytes=VMEM_LIMIT_BYTES)


def _cast_kernel(w_ref, o_ref, *, scale):
    w = w_ref[...]
    if scale != 1.0:
        w = w * scale
    if len(o_ref.shape) == 2:
        o_ref[...] = w.astype(BF16)
    else:
        tc = o_ref.shape[2]
        for t in range(o_ref.shape[0]):
            o_ref[t] = w[:, t * tc:(t + 1) * tc].astype(BF16)


def _to_bf16(w, scale=1.0, col_tile=None):
    layers, r, c = w.shape
    row_bytes = c * jnp.dtype(w.dtype).itemsize
    fits = [br for br in range(BF16_ROWS, r + 1, BF16_ROWS) if r % br == 0 and br * row_bytes <= CAST_BLOCK_BYTES]
    br = max(fits) if fits else r
    in_spec = pl.BlockSpec((None, br, c), lambda l, i: (l, i, 0))
    if col_tile is None:
        out_spec, out_shape = in_spec, w.shape
    else:
        tiles = c // col_tile
        out_spec = pl.BlockSpec((None, tiles, br, col_tile), lambda l, i: (l, 0, i, 0))
        out_shape = (layers, tiles, r, col_tile)
    return pl.pallas_call(
        functools.partial(_cast_kernel, scale=scale),
        grid=(layers, r // br),
        in_specs=[in_spec],
        out_specs=out_spec,
        out_shape=jax.ShapeDtypeStruct(out_shape, BF16),
        compiler_params=_params("parallel", "parallel"),
        name="cast_bf16",
    )(w)


def _layer_block(layer, shape, index, **kwargs):
    return pl.BlockSpec((None,) + shape, lambda *g: (layer,) + tuple(index(*g)), **kwargs)


def _ffn_ln_kernel(x_ref, wg_ref, wu_ref, wd_ref, g_ref, b_ref, *rest, emit_bf16):
    if emit_bf16:
        o_ref, ob_ref, xb_ref, acc_ref = rest
    else:
        o_ref, xb_ref, acc_ref = rest
    j = pl.program_id(1)

    @pl.when(j == 0)
    def _():
        xb_ref[...] = x_ref[...].astype(BF16)
        acc_ref[...] = jnp.zeros_like(acc_ref)

    xb = xb_ref[...]
    hg = jnp.dot(xb, wg_ref[...], preferred_element_type=F32)
    hu = jnp.dot(xb, wu_ref[...], preferred_element_type=F32)
    act = (hg * _sigmoid(hg) * hu).astype(BF16)
    acc_ref[...] += jnp.dot(act, wd_ref[...], preferred_element_type=F32)

    @pl.when(j == pl.num_programs(1) - 1)
    def _():
        y = _layer_norm(ALPHA * x_ref[...] + acc_ref[...], g_ref[...], b_ref[...])
        o_ref[...] = y
        if emit_bf16:
            ob_ref[...] = y.astype(BF16)


def _ffn_ln(x, wg, wu, wd, g, b, layer, emit_bf16):
    n, d = x.shape
    n_tiles, tf = wg.shape[1], wg.shape[3]
    tm = min(FFN_ROWS, n)
    rows = pl.BlockSpec((tm, d), lambda i, j: (i, 0))
    return pl.pallas_call(
        functools.partial(_ffn_ln_kernel, emit_bf16=emit_bf16),
        grid=(n // tm, n_tiles),
        in_specs=[
            rows,
            _layer_block(layer, (None, d, tf), lambda i, j: (j, 0, 0)),
            _layer_block(layer, (None, d, tf), lambda i, j: (j, 0, 0)),
            _layer_block(layer, (tf, d), lambda i, j: (j, 0)),
            pl.BlockSpec((1, d), lambda i, j: (0, 0)),
            pl.BlockSpec((1, d), lambda i, j: (0, 0)),
        ],
        out_specs=[rows, rows] if emit_bf16 else rows,
        out_shape=([jax.ShapeDtypeStruct((n, d), F32), jax.ShapeDtypeStruct((n, d), BF16)] if emit_bf16
                   else jax.ShapeDtypeStruct((n, d), F32)),
        scratch_shapes=[pltpu.VMEM((tm, d), BF16), pltpu.VMEM((tm, d), F32)],
        compiler_params=_params("parallel", "arbitrary"),
        name="ffn_ln",
    )(x, wg, wu, wd, g, b)


def _rotary(h, cos, sin_lo, sin_hi):
    half = ROT_DIM // 2
    return (h * cos + pltpu.roll(h, half, 1) * sin_hi
            + pltpu.roll(h, LANES - half, 1) * sin_lo)


def _inproj_kernel(xb_ref, w_ref, cos_ref, slo_ref, shi_ref, u_ref, qt_ref, k_ref, vt_ref, g_ref):
    j = pl.program_id(1)

    def for_chunks(store):
        xb = xb_ref[...]
        for c in range(w_ref.shape[1] // PROJ_CHUNK):
            cols = slice(c * PROJ_CHUNK, (c + 1) * PROJ_CHUNK)
            h = jnp.dot(xb, w_ref[:, cols], preferred_element_type=F32)
            for s in range(PROJ_CHUNK // LANES):
                lo = c * PROJ_CHUNK + s * LANES
                store(slice(lo, lo + LANES), h[:, s * LANES:(s + 1) * LANES])

    def rope(h):
        return _rotary(h, cos_ref[...], slo_ref[...], shi_ref[...])

    @pl.when(j == 0)
    def _():
        def store(cols, h):
            u_ref[:, cols] = h
        for_chunks(store)

    @pl.when(j == 1)
    def _():
        def store(cols, h):
            qt_ref[cols, :] = (rope(h) * Q_SCALE).T.astype(BF16)
        for_chunks(store)

    @pl.when(j == 2)
    def _():
        def store(cols, h):
            k_ref[:, cols] = rope(h).astype(BF16)
        for_chunks(store)

    @pl.when(j == 3)
    def _():
        def store(cols, h):
            base = cols.start // LANES * VT_ROWS
            vt_ref[base:base + LANES, :] = h.T.astype(BF16)
            vt_ref[base + LANES:base + VT_ROWS, :] = jnp.ones((VT_ROWS - LANES, h.shape[0]), BF16)
        for_chunks(store)

    @pl.when(j >= 4)
    def _():
        def store(cols, h):
            g_ref[:, cols] = _sigmoid(h).astype(BF16)
        for_chunks(store)


def _inproj(x, w, cos, slo, shi, layer):
    n, d = x.shape
    tm = min(PROJ_ROWS, n)
    tn = w.shape[3]
    width = w.shape[1] * tn
    n_gate = (width - SSM_WIDTH - 2 * QK_WIDTH - V_WIDTH) // tn
    assert SSM_WIDTH == tn and QK_WIDTH == tn and V_WIDTH == tn
    tab = pl.BlockSpec((tm, LANES), lambda i, j: (i, 0))
    return pl.pallas_call(
        _inproj_kernel,
        grid=(n // tm, width // tn),
        in_specs=[
            pl.BlockSpec((tm, d), lambda i, j: (i, 0)),
            _layer_block(layer, (None, d, tn), lambda i, j: (j, 0, 0)),
            tab, tab, tab,
        ],
        out_specs=[
            pl.BlockSpec((tm, tn), lambda i, j: (i, 0)),
            pl.BlockSpec((tn, tm), lambda i, j: (0, i)),
            pl.BlockSpec((tm, tn), lambda i, j: (i, 0)),
            pl.BlockSpec((N_HEADS * VT_ROWS, tm), lambda i, j: (0, i)),
            pl.BlockSpec((tm, tn), lambda i, j: (i, jnp.clip(j - 4, 0, n_gate - 1))),
        ],
        out_shape=[
            jax.ShapeDtypeStruct((n, SSM_WIDTH), F32),
            jax.ShapeDtypeStruct((QK_WIDTH, n), BF16),
            jax.ShapeDtypeStruct((n, QK_WIDTH), BF16),
            jax.ShapeDtypeStruct((N_HEADS * VT_ROWS, n), BF16),
            jax.ShapeDtypeStruct((n, n_gate * tn), BF16),
        ],
        compiler_params=_params("parallel", "arbitrary"),
        name="inproj",
    )(x, w, cos, slo, shi)


def _ssm_kernel(u_ref, bbd_ref, cbd_ref, pn_re_ref, pn_im_ref, pp_re_ref, pp_im_ref,
                lb_re_ref, lb_im_ref, d_ref, tri_ref, y_ref, car_re_ref, car_im_ref):
    @pl.when(pl.program_id(0) == 0)
    def _():
        car_re_ref[...] = jnp.zeros_like(car_re_ref)
        car_im_ref[...] = jnp.zeros_like(car_im_ref)

    t_len = pn_re_ref.shape[0]
    n_sub = u_ref.shape[0] // t_len
    slab_in = u_ref.shape[1] // SSM_SLABS
    slab_st = pn_re_ref.shape[1] // SSM_SLABS
    u = u_ref[...]
    ub = u.astype(BF16)
    tri = tri_ref[...]
    for s in range(SSM_SLABS):
        cin = slice(s * slab_in, (s + 1) * slab_in)
        cst = slice(s * slab_st, (s + 1) * slab_st)
        bu = jnp.dot(ub[:, cin], bbd_ref[s], preferred_element_type=F32)
        nr, ni = pn_re_ref[:, cst], pn_im_ref[:, cst]
        z = []
        for k in range(n_sub):
            br = bu[k * t_len:(k + 1) * t_len, :slab_st]
            bi = bu[k * t_len:(k + 1) * t_len, slab_st:]
            z.append(jnp.concatenate([br * nr - bi * ni, br * ni + bi * nr], axis=1).astype(BF16))
        cs = jnp.dot(tri, jnp.concatenate(z, axis=0), preferred_element_type=F32)
        lr, li = lb_re_ref[:, cst], lb_im_ref[:, cst]
        pr, pi = pp_re_ref[:, cst], pp_im_ref[:, cst]
        kr, ki = car_re_ref[:, cst], car_im_ref[:, cst]
        states = []
        for k in range(n_sub):
            cr = cs[k * t_len:(k + 1) * t_len, :slab_st] + (lr * kr - li * ki)
            ci = cs[k * t_len:(k + 1) * t_len, slab_st:] + (lr * ki + li * kr)
            sr = cr * pr - ci * pi
            si = cr * pi + ci * pr
            kr, ki = sr[t_len - 1:t_len, :], si[t_len - 1:t_len, :]
            states.append(jnp.concatenate([sr, si], axis=1).astype(BF16))
        car_re_ref[:, cst] = kr
        car_im_ref[:, cst] = ki
        y = jnp.dot(jnp.concatenate(states, axis=0), cbd_ref[s], preferred_element_type=F32)
        y_ref[:, cin] = y + d_ref[:, cin] * u[:, cin]


def _ssm(u, tables, d_skip, tri, layer):
    n, w = u.shape
    rows = tri.shape[0]
    ns = tables[2].shape[2]
    const2 = lambda shape: pl.BlockSpec(shape, lambda i: (0, 0))
    table = lambda a: _layer_block(layer, a.shape[1:], lambda i: (0,) * (a.ndim - 1))
    return pl.pallas_call(
        _ssm_kernel,
        grid=(n // rows,),
        in_specs=[
            pl.BlockSpec((rows, w), lambda i: (i, 0)),
            *[table(a) for a in tables],
            const2((1, w)), const2((rows, rows)),
        ],
        out_specs=pl.BlockSpec((rows, w), lambda i: (i, 0)),
        out_shape=jax.ShapeDtypeStruct((n, w), F32),
        scratch_shapes=[pltpu.VMEM((1, ns), F32), pltpu.VMEM((1, ns), F32)],
        compiler_params=_params("arbitrary"),
        name="ssm_scan",
    )(u, *tables, d_skip, tri)


def _complex_powers(lr, li, count):
    pr, pi = jnp.ones_like(lr), jnp.zeros_like(li)
    cr, ci = lr, li
    rows = 1
    while rows < count:
        pr, pi = (jnp.concatenate([pr, pr * cr - pi * ci], axis=0),
                  jnp.concatenate([pi, pr * ci + pi * cr], axis=0))
        cr, ci = cr * cr - ci * ci, 2.0 * cr * ci
        rows *= 2
    return pr[:count], pi[:count]


def _ssm_tables(a_re, a_im, b_re, b_im, c_re, c_im, log_dt):
    g, p = a_re.shape
    h = b_re.shape[2]
    dt = jnp.exp(log_dt)[:, None]
    mag = jnp.exp(a_re * dt)
    lb_re = mag * jnp.cos(a_im * dt)
    lb_im = mag * jnp.sin(a_im * dt)
    nr, ni = lb_re - 1.0, lb_im
    den = a_re * a_re + a_im * a_im
    coef_re = (nr * a_re + ni * a_im) / den
    coef_im = (ni * a_re - nr * a_im) / den
    bb_re = coef_re[:, :, None] * b_re - coef_im[:, :, None] * b_im
    bb_im = coef_re[:, :, None] * b_im + coef_im[:, :, None] * b_re
    gs = g // SSM_SLABS
    eye = jnp.eye(gs, dtype=F32)

    def in_blocks(bb):
        t = bb.reshape(SSM_SLABS, gs, p, h).transpose(0, 1, 3, 2)
        return jnp.einsum("ab,sbhp->sahbp", eye, t).reshape(SSM_SLABS, gs * h, gs * p)

    def out_blocks(cc):
        t = cc.reshape(SSM_SLABS, gs, h, p).transpose(0, 1, 3, 2)
        return jnp.einsum("ab,sbph->sbpah", eye, t).reshape(SSM_SLABS, gs * p, gs * h)

    bbd = jnp.concatenate([in_blocks(bb_re), in_blocks(bb_im)], axis=2).astype(BF16)
    cbd = jnp.concatenate([out_blocks(c_re), -out_blocks(c_im)], axis=1).astype(BF16)
    lr, li = lb_re.reshape(1, g * p), lb_im.reshape(1, g * p)
    inv_den = lr * lr + li * li
    pp_re, pp_im = _complex_powers(lr, li, SSM_CHUNK)
    pn_re, pn_im = _complex_powers(lr / inv_den, -li / inv_den, SSM_CHUNK)
    return bbd, cbd, pn_re, pn_im, pp_re, pp_im, lr, li


def _attn_kernel(lam_ref, qt_ref, k_ref, vt_ref, g_ref, o_ref,
                 qz_ref, sa_ref, sb_ref, mxa_ref, mxb_ref, m_ref, acc_ref, *, tk, out_scale):
    i = pl.program_id(1)
    tq = qt_ref.shape[1]
    assert tq == tk
    qt = qt_ref[...]
    comp = lax.broadcasted_iota(jnp.int32, qt.shape, 0) < HEAD_DIM
    zero = jnp.zeros_like(qt)
    qz_ref[0] = jnp.where(comp, qt, zero)
    qz_ref[1] = jnp.where(comp, zero, qt)
    m_ref[...] = jnp.full_like(m_ref, NEG_INF)
    acc_ref[...] = jnp.zeros_like(acc_ref)

    def scores(j, s_ref, mx_ref, masked):
        start = pl.multiple_of(j * tk, tk)
        kb = k_ref[pl.ds(start, tk), :]
        for c in range(2):
            s = jnp.dot(kb, qz_ref[c], preferred_element_type=F32)
            if masked:
                key = lax.broadcasted_iota(jnp.int32, s.shape, 0)
                qry = lax.broadcasted_iota(jnp.int32, s.shape, 1)
                s = jnp.where(key <= qry, s, NEG_INF)
            s_ref[c] = s
            mx_ref[c] = jnp.max(s, axis=0, keepdims=True)

    def update(j, s_ref, mx_ref):
        start = pl.multiple_of(j * tk, tk)
        vtb = vt_ref[:, pl.ds(start, tk)]
        for c in range(2):
            m_prev = m_ref[c]
            m_new = jnp.maximum(m_prev, mx_ref[c])
            alpha = jnp.exp2(m_prev - m_new)
            p = jnp.exp2(s_ref[c] - m_new).astype(BF16)
            acc_ref[c] = alpha * acc_ref[c] + jnp.dot(vtb, p, preferred_element_type=F32)
            m_ref[c] = m_new

    @pl.when(i == 0)
    def _():
        scores(0, sa_ref, mxa_ref, True)
        update(0, sa_ref, mxa_ref)

    @pl.when(i > 0)
    def _():
        scores(0, sa_ref, mxa_ref, False)

    def pair(jj, carry):
        a = 2 * jj
        scores(a + 1, sb_ref, mxb_ref, False)
        update(a, sa_ref, mxa_ref)
        scores(a + 2, sa_ref, mxa_ref, False)
        update(a + 1, sb_ref, mxb_ref)
        return carry

    n_pairs = lax.shift_right_logical(jnp.maximum(i - 1, 0), 1)
    lax.fori_loop(0, n_pairs, pair, 0)
    first = 2 * n_pairs

    @pl.when(i % 2 == 1)
    def _():
        scores(i, sb_ref, mxb_ref, True)
        update(first, sa_ref, mxa_ref)
        update(i, sb_ref, mxb_ref)

    @pl.when((i % 2 == 0) & (i > 0))
    def _():
        scores(first + 1, sb_ref, mxb_ref, False)
        update(first, sa_ref, mxa_ref)
        scores(i, sa_ref, mxa_ref, True)
        update(first + 1, sb_ref, mxb_ref)
        update(i, sa_ref, mxa_ref)

    hw = 2 * HEAD_DIM
    o1 = acc_ref[0, :hw, :] / acc_ref[0, hw:hw + 1, :]
    o2 = acc_ref[1, :hw, :] / acc_ref[1, hw:hw + 1, :]
    ot = o1 - lam_ref[0, 0] * o2
    ms = jnp.mean(ot * ot, axis=0, keepdims=True)
    o = (ot * lax.rsqrt(ms + RMS_EPS)).T
    o_ref[...] = (o * g_ref[...] * out_scale).astype(o_ref.dtype)


def _attention(qt, k, vt, lam, subln_g, lam_init):
    n = k.shape[0]
    tq = min(ATTN_Q, n)
    tk = min(ATTN_K, n)
    hw = 2 * HEAD_DIM
    kernel = functools.partial(_attn_kernel, tk=tk, out_scale=1.0 - lam_init)
    return pl.pallas_call(
        kernel,
        grid=(N_HEADS, n // tq),
        in_specs=[
            pl.BlockSpec(memory_space=pltpu.SMEM),
            pl.BlockSpec((hw, tq), lambda h, i: (h, i)),
            pl.BlockSpec((n, hw), lambda h, i: (0, h)),
            pl.BlockSpec((VT_ROWS, n), lambda h, i: (h, 0)),
            pl.BlockSpec((1, hw), lambda h, i: (0, 0)),
        ],
        out_specs=pl.BlockSpec((tq, hw), lambda h, i: (i, h)),
        out_shape=jax.ShapeDtypeStruct((n, V_WIDTH), BF16),
        scratch_shapes=[
            pltpu.VMEM((2, hw, tq), BF16),
            pltpu.VMEM((2, tk, tq), F32), pltpu.VMEM((2, tk, tq), F32),
            pltpu.VMEM((2, 1, tq), F32), pltpu.VMEM((2, 1, tq), F32),
            pltpu.VMEM((2, 1, tq), F32),
            pltpu.VMEM((2, VT_ROWS, tq), F32),
        ],
        compiler_params=_params("parallel", "arbitrary"),
        name="diff_attn",
    )(lam, qt, k, vt, subln_g)


def _mix_kernel(x_ref, ys_ref, at_ref, sg_ref, wglu_ref, wbs_ref, wba_ref, wout_ref,
                g_ref, b_ref, o_ref):
    d = x_ref.shape[1]
    y = ys_ref[...]
    z = 0.5 * y * (1.0 + lax.erf(y * (2.0 ** -0.5)))
    gate = _sigmoid(jnp.dot(z.astype(BF16), wglu_ref[...], preferred_element_type=F32))
    glu = (z * gate).astype(BF16)
    ya = jnp.dot(glu, wbs_ref[...], preferred_element_type=F32)
    yb = jnp.dot(at_ref[...], wba_ref[...], preferred_element_type=F32)
    sg = sg_ref[...].astype(F32)
    mix = (sg[:, :d] * ya + sg[:, d:] * yb).astype(BF16)
    out = ALPHA * x_ref[...] + jnp.dot(mix, wout_ref[...], preferred_element_type=F32)
    o_ref[...] = _layer_norm(out, g_ref[...], b_ref[...])


def _resident(shape):
    return pl.BlockSpec(shape, lambda i: (0,) * len(shape), pipeline_mode=pl.Buffered(1))


def _resident_layer(w, layer):
    return _layer_block(layer, w.shape[1:], lambda i: (0, 0), pipeline_mode=pl.Buffered(1))


def _mix(x, ys, attn, sg, wglu, wbs, wba, wout, g, b, layer):
    n, d = x.shape
    tm = min(MIX_ROWS, n)
    rows = lambda w: pl.BlockSpec((tm, w), lambda i: (i, 0))
    return pl.pallas_call(
        _mix_kernel,
        grid=(n // tm,),
        in_specs=[
            rows(d), rows(ys.shape[1]), rows(attn.shape[1]), rows(sg.shape[1]),
            _resident_layer(wglu, layer), _resident_layer(wbs, layer), _resident_layer(wba, layer),
            _resident_layer(wout, layer),
            _resident((1, d)), _resident((1, d)),
        ],
        out_specs=rows(d),
        out_shape=jax.ShapeDtypeStruct((n, d), F32),
        compiler_params=_params("parallel"),
        name="mixer_out",
    )(x, ys, attn, sg, wglu, wbs, wba, wout, g, b)


def _ple_kernel(x_ref, p_ref, wg_ref, wp_ref, g_ref, b_ref, o_ref):
    x = x_ref[...]
    gate = _sigmoid(jnp.dot(x.astype(BF16), wg_ref[...], preferred_element_type=F32))
    emb = jnp.dot(p_ref[...].astype(BF16), wp_ref[...], preferred_element_type=F32)
    o_ref[...] = _layer_norm(ALPHA * x + gate * emb, g_ref[...], b_ref[...])


def _ple(x, p, wg, wp, g, b, layer):
    n, d = x.shape
    tm = min(PLE_ROWS, n)
    rows = lambda w: pl.BlockSpec((tm, w), lambda i: (i, 0))
    return pl.pallas_call(
        _ple_kernel,
        grid=(n // tm,),
        in_specs=[rows(d), pl.BlockSpec((None, None, tm, p.shape[3]), lambda i: (layer, 0, i, 0)),
                  _resident_layer(wg, layer), _resident_layer(wp, layer),
                  _resident((1, d)), _resident((1, d))],
        out_specs=rows(d),
        out_shape=jax.ShapeDtypeStruct((n, d), F32),
        compiler_params=_params("parallel"),
        name="ple_ln",
    )(x, p, wg, wp, g, b)


def _rope_tables(positions):
    half = ROT_DIM // 2
    inv_freq = ROPE_THETA ** (-jnp.arange(0, ROT_DIM, 2, dtype=F32) / ROT_DIM)
    ang = positions.astype(F32)[:, None] * inv_freq
    cos, sin = jnp.cos(ang), jnp.sin(ang)
    n = positions.shape[0]
    pad = jnp.zeros((n, HEAD_DIM - ROT_DIM), F32)
    zeros = jnp.zeros((n, half), F32)
    cos_t = jnp.concatenate([cos, cos, pad + 1.0], axis=1)
    sin_lo = jnp.concatenate([-sin, zeros, pad], axis=1)
    sin_hi = jnp.concatenate([zeros, sin, pad], axis=1)
    reps = LANES // HEAD_DIM
    return tuple(jnp.tile(t, (1, reps)) for t in (cos_t, sin_lo, sin_hi))


def kernel(x, p, positions, ffn1_w_gate, ffn1_w_up, ffn1_w_down, ln1_g, ln1_b, w_in, ssm_a_re, ssm_a_im, ssm_b_re, ssm_b_im, ssm_c_re, ssm_c_im, ssm_log_dt, ssm_d, ssm_w_glu, w_branch_ssm, lambda_q1, lambda_k1, lambda_q2, lambda_k2, attn_subln_g, w_branch_attn, w_out, ln2_g, ln2_b, ffn2_w_gate, ffn2_w_up, ffn2_w_down, ln3_g, ln3_b, ple_w_gate, ple_w_proj, ln4_g, ln4_b):
    bsz, length, d = x.shape
    assert bsz == 1
    xs = x.reshape(length, d)
    cos, slo, shi = _rope_tables(positions[0])
    tri = jnp.kron(jnp.eye(SSM_ROWS // SSM_CHUNK, dtype=F32),
                   jnp.tril(jnp.ones((SSM_CHUNK, SSM_CHUNK), F32))).astype(BF16)
    row = lambda a: a.reshape(1, -1)
    w_ffn1 = (_to_bf16(ffn1_w_gate, col_tile=FFN_COLS), _to_bf16(ffn1_w_up, col_tile=FFN_COLS),
              _to_bf16(ffn1_w_down, 0.5))
    w_ffn2 = (_to_bf16(ffn2_w_gate, col_tile=FFN_COLS), _to_bf16(ffn2_w_up, col_tile=FFN_COLS),
              _to_bf16(ffn2_w_down, 0.5))
    w_inb = _to_bf16(w_in, col_tile=PROJ_COLS)
    w_mix = (_to_bf16(ssm_w_glu), _to_bf16(w_branch_ssm), _to_bf16(w_branch_attn), _to_bf16(w_out))
    w_ple = (_to_bf16(ple_w_gate), _to_bf16(ple_w_proj))
    tables = jax.vmap(_ssm_tables)(ssm_a_re, ssm_a_im, ssm_b_re, ssm_b_im, ssm_c_re, ssm_c_im, ssm_log_dt)
    for i in range(DEPTH):
        lam_init = 0.8 - 0.6 * math.exp(-0.3 * i)
        lam = (jnp.exp(jnp.sum(lambda_q1[i] * lambda_k1[i])) - jnp.exp(jnp.sum(lambda_q2[i] * lambda_k2[i]))
               + lam_init).reshape(1, 1)
        xs, xb = _ffn_ln(xs, *w_ffn1, row(ln1_g[i]), row(ln1_b[i]), layer=i, emit_bf16=True)
        u, qt, k, vt, sg = _inproj(xb, w_inb, cos, slo, shi, layer=i)
        ys = _ssm(u, tables, row(ssm_d[i]), tri, layer=i)
        attn = _attention(qt, k, vt, lam, row(attn_subln_g[i]), lam_init)
        xs = _mix(xs, ys, attn, sg, *w_mix, row(ln2_g[i]), row(ln2_b[i]), layer=i)
        xs = _ffn_ln(xs, *w_ffn2, row(ln3_g[i]), row(ln3_b[i]), layer=i, emit_bf16=False)
        xs = _ple(xs, p, *w_ple, row(ln4_g[i]), row(ln4_b[i]), layer=i)
    return xs.reshape(bsz, length, d)
```

```python
import functools
import math

import jax
import jax.numpy as jnp
from jax import lax
from jax.experimental import pallas as pl
from jax.experimental.pallas import tpu as pltpu

F32 = jnp.float32
BF16 = jnp.bfloat16

DEPTH = 2
SSM_WIDTH = 1024
N_HEADS = 8
HEAD_DIM = 64
QK_WIDTH = N_HEADS * 2 * HEAD_DIM
V_WIDTH = N_HEADS * 2 * HEAD_DIM
ROT_DIM = HEAD_DIM // 4
ROPE_THETA = 500000.0
LN_EPS = 1e-5
RMS_EPS = 1e-5
NEG_INF = -1e30
ALPHA = (2 * DEPTH) ** 0.25
Q_SCALE = HEAD_DIM ** -0.5 * math.log2(math.e)

LANES = 128
BF16_ROWS = 16
VMEM_BYTES = 64 * 1024 * 1024
VMEM_LIMIT_BYTES = VMEM_BYTES - 8 * 1024 * 1024
VT_ROWS = 2 * HEAD_DIM + BF16_ROWS
CAST_BLOCK_BYTES = 6 * 1024 * 1024

FFN_ROWS = 512
FFN_COLS = 512
PROJ_ROWS = 1024
PROJ_COLS = 1024
PROJ_CHUNK = 256
SSM_CHUNK = 128
SSM_ROWS = 256
SSM_SLABS = 4
ATTN_Q = 1024
ATTN_K = 1024
ATTN_QCHUNK = 256
MIX_ROWS = 512
PLE_ROWS = 512


def _sigmoid(x):
    return 1.0 / (1.0 + jnp.exp(-x))


def _layer_norm(y, g, b):
    mu = jnp.mean(y, axis=-1, keepdims=True)
    yc = y - mu
    var = jnp.mean(yc * yc, axis=-1, keepdims=True)
    return yc * lax.rsqrt(var + LN_EPS) * g + b


def _params(*semantics):
    return pltpu.CompilerParams(dimension_semantics=semantics, vmem_limit_bytes=VMEM_LIMIT_BYTES)


def _cast_kernel(w_ref, o_ref, *, scale):
    w = w_ref[...]
    if scale != 1.0:
        w = w * scale
    if len(o_ref.shape) == 2:
        o_ref[...] = w.astype(BF16)
    else:
        tc = o_ref.shape[2]
        for t in range(o_ref.shape[0]):
            o_ref[t] = w[:, t * tc:(t + 1) * tc].astype(BF16)


def _to_bf16(w, scale=1.0, col_tile=None):
    layers, r, c = w.shape
    row_bytes = c * jnp.dtype(w.dtype).itemsize
    fits = [br for br in range(BF16_ROWS, r + 1, BF16_ROWS) if r % br == 0 and br * row_bytes <= CAST_BLOCK_BYTES]
    br = max(fits) if fits else r
    in_spec = pl.BlockSpec((None, br, c), lambda l, i: (l, i, 0))
    if col_tile is None:
        out_spec, out_shape = in_spec, w.shape
    else:
        tiles = c // col_tile
        out_spec = pl.BlockSpec((None, tiles, br, col_tile), lambda l, i: (l, 0, i, 0))
        out_shape = (layers, tiles, r, col_tile)
    return pl.pallas_call(
        functools.partial(_cast_kernel, scale=scale),
        grid=(layers, r // br),
        in_specs=[in_spec],
        out_specs=out_spec,
        out_shape=jax.ShapeDtypeStruct(out_shape, BF16),
        compiler_params=_params("parallel", "parallel"),
        name="cast_bf16",
    )(w)


def _layer_block(layer, shape, index, **kwargs):
    return pl.BlockSpec((None,) + shape, lambda *g: (layer,) + tuple(index(*g)), **kwargs)


def _ffn_ln_kernel(x_ref, wg_ref, wu_ref, wd_ref, g_ref, b_ref, *rest, emit_bf16):
    if emit_bf16:
        o_ref, ob_ref, xb_ref, acc_ref = rest
    else:
        o_ref, xb_ref, acc_ref = rest
    j = pl.program_id(1)

    @pl.when(j == 0)
    def _():
        xb_ref[...] = x_ref[...].astype(BF16)
        acc_ref[...] = jnp.zeros_like(acc_ref)

    xb = xb_ref[...]
    hg = jnp.dot(xb, wg_ref[...], preferred_element_type=F32)
    hu = jnp.dot(xb, wu_ref[...], preferred_element_type=F32)
    act = (hg * _sigmoid(hg) * hu).astype(BF16)
    acc_ref[...] += jnp.dot(act, wd_ref[...], preferred_element_type=F32)

    @pl.when(j == pl.num_programs(1) - 1)
    def _():
        y = _layer_norm(ALPHA * x_ref[...] + acc_ref[...], g_ref[...], b_ref[...])
        o_ref[...] = y
        if emit_bf16:
            ob_ref[...] = y.astype(BF16)


def _ffn_ln(x, wg, wu, wd, g, b, layer, emit_bf16):
    n, d = x.shape
    n_tiles, tf = wg.shape[1], wg.shape[3]
    tm = min(FFN_ROWS, n)
    rows = pl.BlockSpec((tm, d), lambda i, j: (i, 0))
    return pl.pallas_call(
        functools.partial(_ffn_ln_kernel, emit_bf16=emit_bf16),
        grid=(n // tm, n_tiles),
        in_specs=[
            rows,
            _layer_block(layer, (None, d, tf), lambda i, j: (j, 0, 0)),
            _layer_block(layer, (None, d, tf), lambda i, j: (j, 0, 0)),
            _layer_block(layer, (tf, d), lambda i, j: (j, 0)),
            pl.BlockSpec((1, d), lambda i, j: (0, 0)),
            pl.BlockSpec((1, d), lambda i, j: (0, 0)),
        ],
        out_specs=[rows, rows] if emit_bf16 else rows,
        out_shape=([jax.ShapeDtypeStruct((n, d), F32), jax.ShapeDtypeStruct((n, d), BF16)] if emit_bf16
                   else jax.ShapeDtypeStruct((n, d), F32)),
        scratch_shapes=[pltpu.VMEM((tm, d), BF16), pltpu.VMEM((tm, d), F32)],
        compiler_params=_params("parallel", "arbitrary"),
        name="ffn_ln",
    )(x, wg, wu, wd, g, b)


def _rotary(h, cos, sin_lo, sin_hi):
    half = ROT_DIM // 2
    return (h * cos + pltpu.roll(h, half, 1) * sin_hi
            + pltpu.roll(h, LANES - half, 1) * sin_lo)


def _inproj_kernel(xb_ref, w_ref, cos_ref, slo_ref, shi_ref, u_ref, qt_ref, k_ref, vt_ref, g_ref):
    j = pl.program_id(1)

    def for_chunks(store):
        xb = xb_ref[...]
        for c in range(w_ref.shape[1] // PROJ_CHUNK):
            cols = slice(c * PROJ_CHUNK, (c + 1) * PROJ_CHUNK)
            h = jnp.dot(xb, w_ref[:, cols], preferred_element_type=F32)
            for s in range(PROJ_CHUNK // LANES):
                lo = c * PROJ_CHUNK + s * LANES
                store(slice(lo, lo + LANES), h[:, s * LANES:(s + 1) * LANES])

    def rope(h):
        return _rotary(h, cos_ref[...], slo_ref[...], shi_ref[...])

    @pl.when(j == 0)
    def _():
        def store(cols, h):
            u_ref[:, cols] = h
        for_chunks(store)

    @pl.when(j == 1)
    def _():
        def store(cols, h):
            qt_ref[cols, :] = (rope(h) * Q_SCALE).T.astype(BF16)
        for_chunks(store)

    @pl.when(j == 2)
    def _():
        def store(cols, h):
            k_ref[:, cols] = rope(h).astype(BF16)
        for_chunks(store)

    @pl.when(j == 3)
    def _():
        def store(cols, h):
            base = cols.start // LANES * VT_ROWS
            vt_ref[base:base + LANES, :] = h.T.astype(BF16)
            vt_ref[base + LANES:base + VT_ROWS, :] = jnp.ones((VT_ROWS - LANES, h.shape[0]), BF16)
        for_chunks(store)

    @pl.when(j >= 4)
    def _():
        def store(cols, h):
            g_ref[:, cols] = _sigmoid(h).astype(BF16)
        for_chunks(store)


def _inproj(x, w, cos, slo, shi, layer):
    n, d = x.shape
    tm = min(PROJ_ROWS, n)
    tn = w.shape[3]
    width = w.shape[1] * tn
    n_gate = (width - SSM_WIDTH - 2 * QK_WIDTH - V_WIDTH) // tn
    assert SSM_WIDTH == tn and QK_WIDTH == tn and V_WIDTH == tn
    tab = pl.BlockSpec((tm, LANES), lambda i, j: (i, 0))
    return pl.pallas_call(
        _inproj_kernel,
        grid=(n // tm, width // tn),
        in_specs=[
            pl.BlockSpec((tm, d), lambda i, j: (i, 0)),
            _layer_block(layer, (None, d, tn), lambda i, j: (j, 0, 0)),
            tab, tab, tab,
        ],
        out_specs=[
            pl.BlockSpec((tm, tn), lambda i, j: (i, 0)),
            pl.BlockSpec((tn, tm), lambda i, j: (0, i)),
            pl.BlockSpec((tm, tn), lambda i, j: (i, 0)),
            pl.BlockSpec((N_HEADS * VT_ROWS, tm), lambda i, j: (0, i)),
            pl.BlockSpec((tm, tn), lambda i, j: (i, jnp.clip(j - 4, 0, n_gate - 1))),
        ],
        out_shape=[
            jax.ShapeDtypeStruct((n, SSM_WIDTH), F32),
            jax.ShapeDtypeStruct((QK_WIDTH, n), BF16),
            jax.ShapeDtypeStruct((n, QK_WIDTH), BF16),
            jax.ShapeDtypeStruct((N_HEADS * VT_ROWS, n), BF16),
            jax.ShapeDtypeStruct((n, n_gate * tn), BF16),
        ],
        compiler_params=_params("parallel", "arbitrary"),
        name="inproj",
    )(x, w, cos, slo, shi)


def _ssm_kernel(u_ref, bbd_ref, cbd_ref, pn_re_ref, pn_im_ref, pp_re_ref, pp_im_ref,
                lb_re_ref, lb_im_ref, d_ref, tri_ref, y_ref, car_re_ref, car_im_ref):
    @pl.when(pl.program_id(0) == 0)
    def _():
        car_re_ref[...] = jnp.zeros_like(car_re_ref)
        car_im_ref[...] = jnp.zeros_like(car_im_ref)

    t_len = pn_re_ref.shape[0]
    n_sub = u_ref.shape[0] // t_len
    slab_in = u_ref.shape[1] // SSM_SLABS
    slab_st = pn_re_ref.shape[1] // SSM_SLABS
    u = u_ref[...]
    ub = u.astype(BF16)
    tri = tri_ref[...]
    for s in range(SSM_SLABS):
        cin = slice(s * slab_in, (s + 1) * slab_in)
        cst = slice(s * slab_st, (s + 1) * slab_st)
        bu = jnp.dot(ub[:, cin], bbd_ref[s], preferred_element_type=F32)
        nr, ni = pn_re_ref[:, cst], pn_im_ref[:, cst]
        z = []
        for k in range(n_sub):
            br = bu[k * t_len:(k + 1) * t_len, :slab_st]
            bi = bu[k * t_len:(k + 1) * t_len, slab_st:]
            z.append(jnp.concatenate([br * nr - bi * ni, br * ni + bi * nr], axis=1).astype(BF16))
        cs = jnp.dot(tri, jnp.concatenate(z, axis=0), preferred_element_type=F32)
        lr, li = lb_re_ref[:, cst], lb_im_ref[:, cst]
        pr, pi = pp_re_ref[:, cst], pp_im_ref[:, cst]
        kr, ki = car_re_ref[:, cst], car_im_ref[:, cst]
        states = []
        for k in range(n_sub):
            cr = cs[k * t_len:(k + 1) * t_len, :slab_st] + (lr * kr - li * ki)
            ci = cs[k * t_len:(k + 1) * t_len, slab_st:] + (lr * ki + li * kr)
            sr = cr * pr - ci * pi
            si = cr * pi + ci * pr
            kr, ki = sr[t_len - 1:t_len, :], si[t_len - 1:t_len, :]
            states.append(jnp.concatenate([sr, si], axis=1).astype(BF16))
        car_re_ref[:, cst] = kr
        car_im_ref[:, cst] = ki
        y = jnp.dot(jnp.concatenate(states, axis=0), cbd_ref[s], preferred_element_type=F32)
        y_ref[:, cin] = y + d_ref[:, cin] * u[:, cin]


def _ssm(u, tables, d_skip, tri, layer):
    n, w = u.shape
    rows = tri.shape[0]
    ns = tables[2].shape[2]
    const2 = lambda shape: pl.BlockSpec(shape, lambda i: (0, 0))
    table = lambda a: _layer_block(layer, a.shape[1:], lambda i: (0,) * (a.ndim - 1))
    return pl.pallas_call(
        _ssm_kernel,
        grid=(n // rows,),
        in_specs=[
            pl.BlockSpec((rows, w), lambda i: (i, 0)),
            *[table(a) for a in tables],
            const2((1, w)), const2((rows, rows)),
        ],
        out_specs=pl.BlockSpec((rows, w), lambda i: (i, 0)),
        out_shape=jax.ShapeDtypeStruct((n, w), F32),
        scratch_shapes=[pltpu.VMEM((1, ns), F32), pltpu.VMEM((1, ns), F32)],
        compiler_params=_params("arbitrary"),
        name="ssm_scan",
    )(u, *tables, d_skip, tri)


def _complex_powers(lr, li, count):
    pr, pi = jnp.ones_like(lr), jnp.zeros_like(li)
    cr, ci = lr, li
    rows = 1
    while rows < count:
        pr, pi = (jnp.concatenate([pr, pr * cr - pi * ci], axis=0),
                  jnp.concatenate([pi, pr * ci + pi * cr], axis=0))
        cr, ci = cr * cr - ci * ci, 2.0 * cr * ci
        rows *= 2
    return pr[:count], pi[:count]


def _ssm_tables(a_re, a_im, b_re, b_im, c_re, c_im, log_dt):
    g, p = a_re.shape
    h = b_re.shape[2]
    dt = jnp.exp(log_dt)[:, None]
    mag = jnp.exp(a_re * dt)
    lb_re = mag * jnp.cos(a_im * dt)
    lb_im = mag * jnp.sin(a_im * dt)
    nr, ni = lb_re - 1.0, lb_im
    den = a_re * a_re + a_im * a_im
    coef_re = (nr * a_re + ni * a_im) / den
    coef_im = (ni * a_re - nr * a_im) / den
    bb_re = coef_re[:, :, None] * b_re - coef_im[:, :, None] * b_im
    bb_im = coef_re[:, :, None] * b_im + coef_im[:, :, None] * b_re
    gs = g // SSM_SLABS
    eye = jnp.eye(gs, dtype=F32)

    def in_blocks(bb):
        t = bb.reshape(SSM_SLABS, gs, p, h).transpose(0, 1, 3, 2)
        return jnp.einsum("ab,sbhp->sahbp", eye, t).reshape(SSM_SLABS, gs * h, gs * p)

    def out_blocks(cc):
        t = cc.reshape(SSM_SLABS, gs, h, p).transpose(0, 1, 3, 2)
        return jnp.einsum("ab,sbph->sbpah", eye, t).reshape(SSM_SLABS, gs * p, gs * h)

    bbd = jnp.concatenate([in_blocks(bb_re), in_blocks(bb_im)], axis=2).astype(BF16)
    cbd = jnp.concatenate([out_blocks(c_re), -out_blocks(c_im)], axis=1).astype(BF16)
    lr, li = lb_re.reshape(1, g * p), lb_im.reshape(1, g * p)
    inv_den = lr * lr + li * li
    pp_re, pp_im = _complex_powers(lr, li, SSM_CHUNK)
    pn_re, pn_im = _complex_powers(lr / inv_den, -li / inv_den, SSM_CHUNK)
    return bbd, cbd, pn_re, pn_im, pp_re, pp_im, lr, li


def _attn_kernel(lam_ref, qt_ref, k_ref, vt_ref, g_ref, o_ref,
                 qz_ref, sa_ref, sb_ref, mxa_ref, mxb_ref, m_ref, acc_ref, *, tk, out_scale):
    i = pl.program_id(1)
    tq = qt_ref.shape[1]
    assert tq == tk
    qt = qt_ref[...]
    comp = lax.broadcasted_iota(jnp.int32, qt.shape, 0) < HEAD_DIM
    zero = jnp.zeros_like(qt)
    qz_ref[0] = jnp.where(comp, qt, zero)
    qz_ref[1] = jnp.where(comp, zero, qt)
    m_ref[...] = jnp.full_like(m_ref, NEG_INF)
    acc_ref[...] = jnp.zeros_like(acc_ref)

    chunks = [slice(q, q + ATTN_QCHUNK) for q in range(0, tq, ATTN_QCHUNK)]

    def scores_chunk(j, s_ref, mx_ref, masked, cols):
        start = pl.multiple_of(j * tk, tk)
        kb = k_ref[pl.ds(start, tk), :]
        for c in range(2):
            s = jnp.dot(kb, qz_ref[c, :, cols], preferred_element_type=F32)
            if masked:
                key = lax.broadcasted_iota(jnp.int32, s.shape, 0)
                qry = lax.broadcasted_iota(jnp.int32, s.shape, 1) + cols.start
                s = jnp.where(key <= qry, s, NEG_INF)
            s_ref[c, :, cols] = s
            mx_ref[c, :, cols] = jnp.max(s, axis=0, keepdims=True)

    def update_chunk(j, s_ref, mx_ref, cols):
        start = pl.multiple_of(j * tk, tk)
        vtb = vt_ref[:, pl.ds(start, tk)]
        for c in range(2):
            m_prev = m_ref[c, :, cols]
            m_new = jnp.maximum(m_prev, mx_ref[c, :, cols])
            alpha = jnp.exp2(m_prev - m_new)
            p = jnp.exp2(s_ref[c, :, cols] - m_new).astype(BF16)
            acc_ref[c, :, cols] = alpha * acc_ref[c, :, cols] + jnp.dot(vtb, p, preferred_element_type=F32)
            m_ref[c, :, cols] = m_new

    def scores(j, s_ref, mx_ref, masked):
        for cols in chunks:
            scores_chunk(j, s_ref, mx_ref, masked, cols)

    def update(j, s_ref, mx_ref):
        for cols in chunks:
            update_chunk(j, s_ref, mx_ref, cols)

    def scores_and_update(j_new, new_slot, masked, j_old, old_slot):
        for cols in chunks:
            scores_chunk(j_new, *new_slot, masked, cols)
            update_chunk(j_old, *old_slot, cols)

    slot_a, slot_b = (sa_ref, mxa_ref), (sb_ref, mxb_ref)

    @pl.when(i == 0)
    def _():
        scores(0, *slot_a, True)
        update(0, *slot_a)

    @pl.when(i > 0)
    def _():
        scores(0, *slot_a, False)

    def pair(jj, carry):
        a = 2 * jj
        scores_and_update(a + 1, slot_b, False, a, slot_a)
        scores_and_update(a + 2, slot_a, False, a + 1, slot_b)
        return carry

    n_pairs = lax.shift_right_logical(jnp.maximum(i - 1, 0), 1)
    lax.fori_loop(0, n_pairs, pair, 0)
    first = 2 * n_pairs

    @pl.when(i % 2 == 1)
    def _():
        scores_and_update(i, slot_b, True, first, slot_a)
        update(i, *slot_b)

    @pl.when((i % 2 == 0) & (i > 0))
    def _():
        scores_and_update(first + 1, slot_b, False, first, slot_a)
        scores_and_update(i, slot_a, True, first + 1, slot_b)
        update(i, *slot_a)

    hw = 2 * HEAD_DIM
    o1 = acc_ref[0, :hw, :] / acc_ref[0, hw:hw + 1, :]
    o2 = acc_ref[1, :hw, :] / acc_ref[1, hw:hw + 1, :]
    ot = o1 - lam_ref[0, 0] * o2
    ms = jnp.mean(ot * ot, axis=0, keepdims=True)
    o = (ot * lax.rsqrt(ms + RMS_EPS)).T
    o_ref[...] = (o * g_ref[...] * out_scale).astype(o_ref.dtype)


def _attention(qt, k, vt, lam, subln_g, lam_init):
    n = k.shape[0]
    tq = min(ATTN_Q, n)
    tk = min(ATTN_K, n)
    hw = 2 * HEAD_DIM
    kernel = functools.partial(_attn_kernel, tk=tk, out_scale=1.0 - lam_init)
    return pl.pallas_call(
        kernel,
        grid=(N_HEADS, n // tq),
        in_specs=[
            pl.BlockSpec(memory_space=pltpu.SMEM),
            pl.BlockSpec((hw, tq), lambda h, i: (h, i)),
            pl.BlockSpec((n, hw), lambda h, i: (0, h)),
            pl.BlockSpec((VT_ROWS, n), lambda h, i: (h, 0)),
            pl.BlockSpec((1, hw), lambda h, i: (0, 0)),
        ],
        out_specs=pl.BlockSpec((tq, hw), lambda h, i: (i, h)),
        out_shape=jax.ShapeDtypeStruct((n, V_WIDTH), BF16),
        scratch_shapes=[
            pltpu.VMEM((2, hw, tq), BF16),
            pltpu.VMEM((2, tk, tq), F32), pltpu.VMEM((2, tk, tq), F32),
            pltpu.VMEM((2, 1, tq), F32), pltpu.VMEM((2, 1, tq), F32),
            pltpu.VMEM((2, 1, tq), F32),
            pltpu.VMEM((2, VT_ROWS, tq), F32),
        ],
        compiler_params=_params("parallel", "arbitrary"),
        name="diff_attn",
    )(lam, qt, k, vt, subln_g)


def _mix_kernel(x_ref, ys_ref, at_ref, sg_ref, wglu_ref, wbs_ref, wba_ref, wout_ref,
                g_ref, b_ref, o_ref):
    d = x_ref.shape[1]
    y = ys_ref[...]
    z = 0.5 * y * (1.0 + lax.erf(y * (2.0 ** -0.5)))
    gate = _sigmoid(jnp.dot(z.astype(BF16), wglu_ref[...], preferred_element_type=F32))
    glu = (z * gate).astype(BF16)
    ya = jnp.dot(glu, wbs_ref[...], preferred_element_type=F32)
    yb = jnp.dot(at_ref[...], wba_ref[...], preferred_element_type=F32)
    sg = sg_ref[...].astype(F32)
    mix = (sg[:, :d] * ya + sg[:, d:] * yb).astype(BF16)
    out = ALPHA * x_ref[...] + jnp.dot(mix, wout_ref[...], preferred_element_type=F32)
    o_ref[...] = _layer_norm(out, g_ref[...], b_ref[...])


def _resident(shape):
    return pl.BlockSpec(shape, lambda i: (0,) * len(shape), pipeline_mode=pl.Buffered(1))


def _resident_layer(w, layer):
    return _layer_block(layer, w.shape[1:], lambda i: (0, 0), pipeline_mode=pl.Buffered(1))


def _mix(x, ys, attn, sg, wglu, wbs, wba, wout, g, b, layer):
    n, d = x.shape
    tm = min(MIX_ROWS, n)
    rows = lambda w: pl.BlockSpec((tm, w), lambda i: (i, 0))
    return pl.pallas_call(
        _mix_kernel,
        grid=(n // tm,),
        in_specs=[
            rows(d), rows(ys.shape[1]), rows(attn.shape[1]), rows(sg.shape[1]),
            _resident_layer(wglu, layer), _resident_layer(wbs, layer), _resident_layer(wba, layer),
            _resident_layer(wout, layer),
            _resident((1, d)), _resident((1, d)),
        ],
        out_specs=rows(d),
        out_shape=jax.ShapeDtypeStruct((n, d), F32),
        compiler_params=_params("parallel"),
        name="mixer_out",
    )(x, ys, attn, sg, wglu, wbs, wba, wout, g, b)


def _ple_kernel(x_ref, p_ref, wg_ref, wp_ref, g_ref, b_ref, o_ref):
    x = x_ref[...]
    gate = _sigmoid(jnp.dot(x.astype(BF16), wg_ref[...], preferred_element_type=F32))
    emb = jnp.dot(p_ref[...].astype(BF16), wp_ref[...], preferred_element_type=F32)
    o_ref[...] = _layer_norm(ALPHA * x + gate * emb, g_ref[...], b_ref[...])


def _ple(x, p, wg, wp, g, b, layer):
    n, d = x.shape
    tm = min(PLE_ROWS, n)
    rows = lambda w: pl.BlockSpec((tm, w), lambda i: (i, 0))
    return pl.pallas_call(
        _ple_kernel,
        grid=(n // tm,),
        in_specs=[rows(d), pl.BlockSpec((None, None, tm, p.shape[3]), lambda i: (layer, 0, i, 0)),
                  _resident_layer(wg, layer), _resident_layer(wp, layer),
                  _resident((1, d)), _resident((1, d))],
        out_specs=rows(d),
        out_shape=jax.ShapeDtypeStruct((n, d), F32),
        compiler_params=_params("parallel"),
        name="ple_ln",
    )(x, p, wg, wp, g, b)


def _rope_tables(positions):
    half = ROT_DIM // 2
    inv_freq = ROPE_THETA ** (-jnp.arange(0, ROT_DIM, 2, dtype=F32) / ROT_DIM)
    ang = positions.astype(F32)[:, None] * inv_freq
    cos, sin = jnp.cos(ang), jnp.sin(ang)
    n = positions.shape[0]
    pad = jnp.zeros((n, HEAD_DIM - ROT_DIM), F32)
    zeros = jnp.zeros((n, half), F32)
    cos_t = jnp.concatenate([cos, cos, pad + 1.0], axis=1)
    sin_lo = jnp.concatenate([-sin, zeros, pad], axis=1)
    sin_hi = jnp.concatenate([zeros, sin, pad], axis=1)
    reps = LANES // HEAD_DIM
    return tuple(jnp.tile(t, (1, reps)) for t in (cos_t, sin_lo, sin_hi))


def kernel(x, p, positions, ffn1_w_gate, ffn1_w_up, ffn1_w_down, ln1_g, ln1_b, w_in, ssm_a_re, ssm_a_im, ssm_b_re, ssm_b_im, ssm_c_re, ssm_c_im, ssm_log_dt, ssm_d, ssm_w_glu, w_branch_ssm, lambda_q1, lambda_k1, lambda_q2, lambda_k2, attn_subln_g, w_branch_attn, w_out, ln2_g, ln2_b, ffn2_w_gate, ffn2_w_up, ffn2_w_down, ln3_g, ln3_b, ple_w_gate, ple_w_proj, ln4_g, ln4_b):
    bsz, length, d = x.shape
    assert bsz == 1
    xs = x.reshape(length, d)
    cos, slo, shi = _rope_tables(positions[0])
    tri = jnp.kron(jnp.eye(SSM_ROWS // SSM_CHUNK, dtype=F32),
                   jnp.tril(jnp.ones((SSM_CHUNK, SSM_CHUNK), F32))).astype(BF16)
    row = lambda a: a.reshape(1, -1)
    w_ffn1 = (_to_bf16(ffn1_w_gate, col_tile=FFN_COLS), _to_bf16(ffn1_w_up, col_tile=FFN_COLS),
              _to_bf16(ffn1_w_down, 0.5))
    w_ffn2 = (_to_bf16(ffn2_w_gate, col_tile=FFN_COLS), _to_bf16(ffn2_w_up, col_tile=FFN_COLS),
              _to_bf16(ffn2_w_down, 0.5))
    w_inb = _to_bf16(w_in, col_tile=PROJ_COLS)
    w_mix = (_to_bf16(ssm_w_glu), _to_bf16(w_branch_ssm), _to_bf16(w_branch_attn), _to_bf16(w_out))
    w_ple = (_to_bf16(ple_w_gate), _to_bf16(ple_w_proj))
    tables = jax.vmap(_ssm_tables)(ssm_a_re, ssm_a_im, ssm_b_re, ssm_b_im, ssm_c_re, ssm_c_im, ssm_log_dt)
    for i in range(DEPTH):
        lam_init = 0.8 - 0.6 * math.exp(-0.3 * i)
        lam = (jnp.exp(jnp.sum(lambda_q1[i] * lambda_k1[i])) - jnp.exp(jnp.sum(lambda_q2[i] * lambda_k2[i]))
               + lam_init).reshape(1, 1)
        xs, xb = _ffn_ln(xs, *w_ffn1, row(ln1_g[i]), row(ln1_b[i]), layer=i, emit_bf16=True)
        u, qt, k, vt, sg = _inproj(xb, w_inb, cos, slo, shi, layer=i)
        ys = _ssm(u, tables, row(ssm_d[i]), tri, layer=i)
        attn = _attention(qt, k, vt, lam, row(attn_subln_g[i]), lam_init)
        xs = _mix(xs, ys, attn, sg, *w_mix, row(ln2_g[i]), row(ln2_b[i]), layer=i)
        xs = _ffn_ln(xs, *w_ffn2, row(ln3_g[i]), row(ln3_b[i]), layer=i, emit_bf16=False)
        xs = _ple(xs, p, *w_ple, row(ln4_g[i]), row(ln4_b[i]), layer=i)
    return xs.reshape(bsz, length, d)
```

```python
import functools
import math

import jax
import jax.numpy as jnp
from jax import lax
from jax.experimental import pallas as pl
from jax.experimental.pallas import tpu as pltpu

F32 = jnp.float32
BF16 = jnp.bfloat16

DEPTH = 2
SSM_WIDTH = 1024
N_HEADS = 8
HEAD_DIM = 64
QK_WIDTH = N_HEADS * 2 * HEAD_DIM
V_WIDTH = N_HEADS * 2 * HEAD_DIM
ROT_DIM = HEAD_DIM // 4
ROPE_THETA = 500000.0
LN_EPS = 1e-5
RMS_EPS = 1e-5
NEG_INF = -1e30
ALPHA = (2 * DEPTH) ** 0.25
Q_SCALE = HEAD_DIM ** -0.5 * math.log2(math.e)

LANES = 128
BF16_ROWS = 16
VMEM_BYTES = 64 * 1024 * 1024
VMEM_LIMIT_BYTES = VMEM_BYTES - 8 * 1024 * 1024
VT_ROWS = 2 * HEAD_DIM + BF16_ROWS
CAST_BLOCK_BYTES = 6 * 1024 * 1024

FFN_ROWS = 512
FFN_COLS = 512
PROJ_ROWS = 1024
PROJ_COLS = 1024
PROJ_CHUNK = 256
SSM_CHUNK = 128
SSM_ROWS = 256
SSM_SLABS = 4
ATTN_Q = 1024
ATTN_K = 1024
ATTN_QCHUNK = 256
MIX_ROWS = 512
PLE_ROWS = 512


def _sigmoid(x):
    return 1.0 / (1.0 + jnp.exp(-x))


def _layer_norm(y, g, b):
    mu = jnp.mean(y, axis=-1, keepdims=True)
    yc = y - mu
    var = jnp.mean(yc * yc, axis=-1, keepdims=True)
    return yc * lax.rsqrt(var + LN_EPS) * g + b


def _params(*semantics):
    return pltpu.CompilerParams(dimension_semantics=semantics, vmem_limit_bytes=VMEM_LIMIT_BYTES)


def _cast_kernel(w_ref, o_ref, *, scale):
    w = w_ref[...]
    if scale != 1.0:
        w = w * scale
    if len(o_ref.shape) == 2:
        o_ref[...] = w.astype(BF16)
    else:
        tc = o_ref.shape[2]
        for t in range(o_ref.shape[0]):
            o_ref[t] = w[:, t * tc:(t + 1) * tc].astype(BF16)


def _to_bf16(w, scale=1.0, col_tile=None):
    layers, r, c = w.shape
    row_bytes = c * jnp.dtype(w.dtype).itemsize
    fits = [br for br in range(BF16_ROWS, r + 1, BF16_ROWS) if r % br == 0 and br * row_bytes <= CAST_BLOCK_BYTES]
    br = max(fits) if fits else r
    in_spec = pl.BlockSpec((None, br, c), lambda l, i: (l, i, 0))
    if col_tile is None:
        out_spec, out_shape = in_spec, w.shape
    else:
        tiles = c // col_tile
        out_spec = pl.BlockSpec((None, tiles, br, col_tile), lambda l, i: (l, 0, i, 0))
        out_shape = (layers, tiles, r, col_tile)
    return pl.pallas_call(
        functools.partial(_cast_kernel, scale=scale),
        grid=(layers, r // br),
        in_specs=[in_spec],
        out_specs=out_spec,
        out_shape=jax.ShapeDtypeStruct(out_shape, BF16),
        compiler_params=_params("parallel", "parallel"),
        name="cast_bf16",
    )(w)


def _layer_block(layer, shape, index, **kwargs):
    return pl.BlockSpec((None,) + shape, lambda *g: (layer,) + tuple(index(*g)), **kwargs)


def _ffn_ln_kernel(x_ref, wg_ref, wu_ref, wd_ref, g_ref, b_ref, *rest, emit_bf16):
    if emit_bf16:
        o_ref, ob_ref, xb_ref, acc_ref = rest
    else:
        o_ref, xb_ref, acc_ref = rest
    j = pl.program_id(1)

    @pl.when(j == 0)
    def _():
        xb_ref[...] = x_ref[...].astype(BF16)
        acc_ref[...] = jnp.zeros_like(acc_ref)

    xb = xb_ref[...]
    hg = jnp.dot(xb, wg_ref[...], preferred_element_type=F32)
    hu = jnp.dot(xb, wu_ref[...], preferred_element_type=F32)
    act = (hg * _sigmoid(hg) * hu).astype(BF16)
    acc_ref[...] += jnp.dot(act, wd_ref[...], preferred_element_type=F32)

    @pl.when(j == pl.num_programs(1) - 1)
    def _():
        y = _layer_norm(ALPHA * x_ref[...] + acc_ref[...], g_ref[...], b_ref[...])
        o_ref[...] = y
        if emit_bf16:
            ob_ref[...] = y.astype(BF16)


def _ffn_ln(x, wg, wu, wd, g, b, layer, emit_bf16):
    n, d = x.shape
    n_tiles, tf = wg.shape[1], wg.shape[3]
    tm = min(FFN_ROWS, n)
    rows = pl.BlockSpec((tm, d), lambda i, j: (i, 0))
    return pl.pallas_call(
        functools.partial(_ffn_ln_kernel, emit_bf16=emit_bf16),
        grid=(n // tm, n_tiles),
        in_specs=[
            rows,
            _layer_block(layer, (None, d, tf), lambda i, j: (j, 0, 0)),
            _layer_block(layer, (None, d, tf), lambda i, j: (j, 0, 0)),
            _layer_block(layer, (tf, d), lambda i, j: (j, 0)),
            pl.BlockSpec((1, d), lambda i, j: (0, 0)),
            pl.BlockSpec((1, d), lambda i, j: (0, 0)),
        ],
        out_specs=[rows, rows] if emit_bf16 else rows,
        out_shape=([jax.ShapeDtypeStruct((n, d), F32), jax.ShapeDtypeStruct((n, d), BF16)] if emit_bf16
                   else jax.ShapeDtypeStruct((n, d), F32)),
        scratch_shapes=[pltpu.VMEM((tm, d), BF16), pltpu.VMEM((tm, d), F32)],
        compiler_params=_params("parallel", "arbitrary"),
        name="ffn_ln",
    )(x, wg, wu, wd, g, b)


def _rotary(h, cos, sin_lo, sin_hi):
    half = ROT_DIM // 2
    return (h * cos + pltpu.roll(h, half, 1) * sin_hi
            + pltpu.roll(h, LANES - half, 1) * sin_lo)


def _inproj_kernel(xb_ref, w_ref, cos_ref, slo_ref, shi_ref, u_ref, qt_ref, k_ref, vt_ref, g_ref):
    j = pl.program_id(1)

    def for_chunks(store):
        xb = xb_ref[...]
        for c in range(w_ref.shape[1] // PROJ_CHUNK):
            cols = slice(c * PROJ_CHUNK, (c + 1) * PROJ_CHUNK)
            h = jnp.dot(xb, w_ref[:, cols], preferred_element_type=F32)
            for s in range(PROJ_CHUNK // LANES):
                lo = c * PROJ_CHUNK + s * LANES
                store(slice(lo, lo + LANES), h[:, s * LANES:(s + 1) * LANES])

    def rope(h):
        return _rotary(h, cos_ref[...], slo_ref[...], shi_ref[...])

    @pl.when(j == 0)
    def _():
        def store(cols, h):
            u_ref[:, cols] = h
        for_chunks(store)

    @pl.when(j == 1)
    def _():
        def store(cols, h):
            qt_ref[cols, :] = (rope(h) * Q_SCALE).T.astype(BF16)
        for_chunks(store)

    @pl.when(j == 2)
    def _():
        def store(cols, h):
            k_ref[:, cols] = rope(h).astype(BF16)
        for_chunks(store)

    @pl.when(j == 3)
    def _():
        def store(cols, h):
            base = cols.start // LANES * VT_ROWS
            vt_ref[base:base + LANES, :] = h.T.astype(BF16)
            vt_ref[base + LANES:base + VT_ROWS, :] = jnp.ones((VT_ROWS - LANES, h.shape[0]), BF16)
        for_chunks(store)

    @pl.when(j >= 4)
    def _():
        def store(cols, h):
            g_ref[:, cols] = _sigmoid(h).astype(BF16)
        for_chunks(store)


def _inproj(x, w, cos, slo, shi, layer):
    n, d = x.shape
    tm = min(PROJ_ROWS, n)
    tn = w.shape[3]
    width = w.shape[1] * tn
    n_gate = (width - SSM_WIDTH - 2 * QK_WIDTH - V_WIDTH) // tn
    assert SSM_WIDTH == tn and QK_WIDTH == tn and V_WIDTH == tn
    tab = pl.BlockSpec((tm, LANES), lambda i, j: (i, 0))
    return pl.pallas_call(
        _inproj_kernel,
        grid=(n // tm, width // tn),
        in_specs=[
            pl.BlockSpec((tm, d), lambda i, j: (i, 0)),
            _layer_block(layer, (None, d, tn), lambda i, j: (j, 0, 0)),
            tab, tab, tab,
        ],
        out_specs=[
            pl.BlockSpec((tm, tn), lambda i, j: (i, 0)),
            pl.BlockSpec((tn, tm), lambda i, j: (0, i)),
            pl.BlockSpec((tm, tn), lambda i, j: (i, 0)),
            pl.BlockSpec((N_HEADS * VT_ROWS, tm), lambda i, j: (0, i)),
            pl.BlockSpec((tm, tn), lambda i, j: (i, jnp.clip(j - 4, 0, n_gate - 1))),
        ],
        out_shape=[
            jax.ShapeDtypeStruct((n, SSM_WIDTH), F32),
            jax.ShapeDtypeStruct((QK_WIDTH, n), BF16),
            jax.ShapeDtypeStruct((n, QK_WIDTH), BF16),
            jax.ShapeDtypeStruct((N_HEADS * VT_ROWS, n), BF16),
            jax.ShapeDtypeStruct((n, n_gate * tn), BF16),
        ],
        compiler_params=_params("parallel", "arbitrary"),
        name="inproj",
    )(x, w, cos, slo, shi)


def _ssm_kernel(u_ref, bbd_ref, cbd_ref, pn_re_ref, pn_im_ref, pp_re_ref, pp_im_ref,
                lb_re_ref, lb_im_ref, d_ref, tri_ref, y_ref, car_re_ref, car_im_ref):
    @pl.when(pl.program_id(0) == 0)
    def _():
        car_re_ref[...] = jnp.zeros_like(car_re_ref)
        car_im_ref[...] = jnp.zeros_like(car_im_ref)

    t_len = pn_re_ref.shape[0]
    n_sub = u_ref.shape[0] // t_len
    slab_in = u_ref.shape[1] // SSM_SLABS
    slab_st = pn_re_ref.shape[1] // SSM_SLABS
    u = u_ref[...]
    ub = u.astype(BF16)
    tri = tri_ref[...]
    def cin(s):
        return slice(s * slab_in, (s + 1) * slab_in)

    def cst(s):
        return slice(s * slab_st, (s + 1) * slab_st)

    def drive(s):
        bu = jnp.dot(ub[:, cin(s)], bbd_ref[s], preferred_element_type=F32)
        nr, ni = pn_re_ref[:, cst(s)], pn_im_ref[:, cst(s)]
        z = []
        for k in range(n_sub):
            br = bu[k * t_len:(k + 1) * t_len, :slab_st]
            bi = bu[k * t_len:(k + 1) * t_len, slab_st:]
            z.append(jnp.concatenate([br * nr - bi * ni, br * ni + bi * nr], axis=1).astype(BF16))
        return jnp.concatenate(z, axis=0)

    def project(s, cs):
        lr, li = lb_re_ref[:, cst(s)], lb_im_ref[:, cst(s)]
        pr, pi = pp_re_ref[:, cst(s)], pp_im_ref[:, cst(s)]
        kr, ki = car_re_ref[:, cst(s)], car_im_ref[:, cst(s)]
        states = []
        for k in range(n_sub):
            cr = cs[k * t_len:(k + 1) * t_len, :slab_st] + (lr * kr - li * ki)
            ci = cs[k * t_len:(k + 1) * t_len, slab_st:] + (lr * ki + li * kr)
            sr = cr * pr - ci * pi
            si = cr * pi + ci * pr
            kr, ki = sr[t_len - 1:t_len, :], si[t_len - 1:t_len, :]
            states.append(jnp.concatenate([sr, si], axis=1).astype(BF16))
        car_re_ref[:, cst(s)] = kr
        car_im_ref[:, cst(s)] = ki
        y = jnp.dot(jnp.concatenate(states, axis=0), cbd_ref[s], preferred_element_type=F32)
        y_ref[:, cin(s)] = y + d_ref[:, cin(s)] * u[:, cin(s)]

    z, cs = {}, {}
    for step in range(SSM_SLABS + 2):
        if step < SSM_SLABS:
            z[step] = drive(step)
        if 0 <= step - 1 < SSM_SLABS:
            cs[step - 1] = jnp.dot(tri, z.pop(step - 1), preferred_element_type=F32)
        if 0 <= step - 2 < SSM_SLABS:
            project(step - 2, cs.pop(step - 2))


def _ssm(u, tables, d_skip, tri, layer):
    n, w = u.shape
    rows = tri.shape[0]
    ns = tables[2].shape[2]
    const2 = lambda shape: pl.BlockSpec(shape, lambda i: (0, 0))
    table = lambda a: _layer_block(layer, a.shape[1:], lambda i: (0,) * (a.ndim - 1))
    return pl.pallas_call(
        _ssm_kernel,
        grid=(n // rows,),
        in_specs=[
            pl.BlockSpec((rows, w), lambda i: (i, 0)),
            *[table(a) for a in tables],
            const2((1, w)), const2((rows, rows)),
        ],
        out_specs=pl.BlockSpec((rows, w), lambda i: (i, 0)),
        out_shape=jax.ShapeDtypeStruct((n, w), F32),
        scratch_shapes=[pltpu.VMEM((1, ns), F32), pltpu.VMEM((1, ns), F32)],
        compiler_params=_params("arbitrary"),
        name="ssm_scan",
    )(u, *tables, d_skip, tri)


def _complex_powers(lr, li, count):
    pr, pi = jnp.ones_like(lr), jnp.zeros_like(li)
    cr, ci = lr, li
    rows = 1
    while rows < count:
        pr, pi = (jnp.concatenate([pr, pr * cr - pi * ci], axis=0),
                  jnp.concatenate([pi, pr * ci + pi * cr], axis=0))
        cr, ci = cr * cr - ci * ci, 2.0 * cr * ci
        rows *= 2
    return pr[:count], pi[:count]


def _ssm_tables(a_re, a_im, b_re, b_im, c_re, c_im, log_dt):
    g, p = a_re.shape
    h = b_re.shape[2]
    dt = jnp.exp(log_dt)[:, None]
    mag = jnp.exp(a_re * dt)
    lb_re = mag * jnp.cos(a_im * dt)
    lb_im = mag * jnp.sin(a_im * dt)
    nr, ni = lb_re - 1.0, lb_im
    den = a_re * a_re + a_im * a_im
    coef_re = (nr * a_re + ni * a_im) / den
    coef_im = (ni * a_re - nr * a_im) / den
    bb_re = coef_re[:, :, None] * b_re - coef_im[:, :, None] * b_im
    bb_im = coef_re[:, :, None] * b_im + coef_im[:, :, None] * b_re
    gs = g // SSM_SLABS
    eye = jnp.eye(gs, dtype=F32)

    def in_blocks(bb):
        t = bb.reshape(SSM_SLABS, gs, p, h).transpose(0, 1, 3, 2)
        return jnp.einsum("ab,sbhp->sahbp", eye, t).reshape(SSM_SLABS, gs * h, gs * p)

    def out_blocks(cc):
        t = cc.reshape(SSM_SLABS, gs, h, p).transpose(0, 1, 3, 2)
        return jnp.einsum("ab,sbph->sbpah", eye, t).reshape(SSM_SLABS, gs * p, gs * h)

    bbd = jnp.concatenate([in_blocks(bb_re), in_blocks(bb_im)], axis=2).astype(BF16)
    cbd = jnp.concatenate([out_blocks(c_re), -out_blocks(c_im)], axis=1).astype(BF16)
    lr, li = lb_re.reshape(1, g * p), lb_im.reshape(1, g * p)
    inv_den = lr * lr + li * li
    pp_re, pp_im = _complex_powers(lr, li, SSM_CHUNK)
    pn_re, pn_im = _complex_powers(lr / inv_den, -li / inv_den, SSM_CHUNK)
    return bbd, cbd, pn_re, pn_im, pp_re, pp_im, lr, li


def _attn_kernel(lam_ref, qt_ref, k_ref, vt_ref, g_ref, o_ref,
                 qz_ref, sa_ref, sb_ref, mxa_ref, mxb_ref, m_ref, acc_ref, *, tk, out_scale):
    i = pl.program_id(1)
    tq = qt_ref.shape[1]
    assert tq == tk
    qt = qt_ref[...]
    comp = lax.broadcasted_iota(jnp.int32, qt.shape, 0) < HEAD_DIM
    zero = jnp.zeros_like(qt)
    qz_ref[0] = jnp.where(comp, qt, zero)
    qz_ref[1] = jnp.where(comp, zero, qt)
    m_ref[...] = jnp.full_like(m_ref, NEG_INF)
    acc_ref[...] = jnp.zeros_like(acc_ref)

    chunks = [slice(q, q + ATTN_QCHUNK) for q in range(0, tq, ATTN_QCHUNK)]

    def scores_chunk(j, s_ref, mx_ref, masked, cols):
        start = pl.multiple_of(j * tk, tk)
        kb = k_ref[pl.ds(start, tk), :]
        for c in range(2):
            s = jnp.dot(kb, qz_ref[c, :, cols], preferred_element_type=F32)
            if masked:
                key = lax.broadcasted_iota(jnp.int32, s.shape, 0)
                qry = lax.broadcasted_iota(jnp.int32, s.shape, 1) + cols.start
                s = jnp.where(key <= qry, s, NEG_INF)
            s_ref[c, :, cols] = s
            mx_ref[c, :, cols] = jnp.max(s, axis=0, keepdims=True)

    def update_chunk(j, s_ref, mx_ref, cols):
        start = pl.multiple_of(j * tk, tk)
        vtb = vt_ref[:, pl.ds(start, tk)]
        for c in range(2):
            m_prev = m_ref[c, :, cols]
            m_new = jnp.maximum(m_prev, mx_ref[c, :, cols])
            alpha = jnp.exp2(m_prev - m_new)
            p = jnp.exp2(s_ref[c, :, cols] - m_new).astype(BF16)
            acc_ref[c, :, cols] = alpha * acc_ref[c, :, cols] + jnp.dot(vtb, p, preferred_element_type=F32)
            m_ref[c, :, cols] = m_new

    def scores(j, s_ref, mx_ref, masked):
        for cols in chunks:
            scores_chunk(j, s_ref, mx_ref, masked, cols)

    def update(j, s_ref, mx_ref):
        for cols in chunks:
            update_chunk(j, s_ref, mx_ref, cols)

    def scores_and_update(j_new, new_slot, masked, j_old, old_slot):
        for cols in chunks:
            scores_chunk(j_new, *new_slot, masked, cols)
            update_chunk(j_old, *old_slot, cols)

    slot_a, slot_b = (sa_ref, mxa_ref), (sb_ref, mxb_ref)

    @pl.when(i == 0)
    def _():
        scores(0, *slot_a, True)
        update(0, *slot_a)

    @pl.when(i > 0)
    def _():
        scores(0, *slot_a, False)

    def pair(jj, carry):
        a = 2 * jj
        scores_and_update(a + 1, slot_b, False, a, slot_a)
        scores_and_update(a + 2, slot_a, False, a + 1, slot_b)
        return carry

    n_pairs = lax.shift_right_logical(jnp.maximum(i - 1, 0), 1)
    lax.fori_loop(0, n_pairs, pair, 0)
    first = 2 * n_pairs

    @pl.when(i % 2 == 1)
    def _():
        scores_and_update(i, slot_b, True, first, slot_a)
        update(i, *slot_b)

    @pl.when((i % 2 == 0) & (i > 0))
    def _():
        scores_and_update(first + 1, slot_b, False, first, slot_a)
        scores_and_update(i, slot_a, True, first + 1, slot_b)
        update(i, *slot_a)

    hw = 2 * HEAD_DIM
    o1 = acc_ref[0, :hw, :] / acc_ref[0, hw:hw + 1, :]
    o2 = acc_ref[1, :hw, :] / acc_ref[1, hw:hw + 1, :]
    ot = o1 - lam_ref[0, 0] * o2
    ms = jnp.mean(ot * ot, axis=0, keepdims=True)
    o = (ot * lax.rsqrt(ms + RMS_EPS)).T
    o_ref[...] = (o * g_ref[...] * out_scale).astype(o_ref.dtype)


def _attention(qt, k, vt, lam, subln_g, lam_init):
    n = k.shape[0]
    tq = min(ATTN_Q, n)
    tk = min(ATTN_K, n)
    hw = 2 * HEAD_DIM
    kernel = functools.partial(_attn_kernel, tk=tk, out_scale=1.0 - lam_init)
    return pl.pallas_call(
        kernel,
        grid=(N_HEADS, n // tq),
        in_specs=[
            pl.BlockSpec(memory_space=pltpu.SMEM),
            pl.BlockSpec((hw, tq), lambda h, i: (h, i)),
            pl.BlockSpec((n, hw), lambda h, i: (0, h)),
            pl.BlockSpec((VT_ROWS, n), lambda h, i: (h, 0)),
            pl.BlockSpec((1, hw), lambda h, i: (0, 0)),
        ],
        out_specs=pl.BlockSpec((tq, hw), lambda h, i: (i, h)),
        out_shape=jax.ShapeDtypeStruct((n, V_WIDTH), BF16),
        scratch_shapes=[
            pltpu.VMEM((2, hw, tq), BF16),
            pltpu.VMEM((2, tk, tq), F32), pltpu.VMEM((2, tk, tq), F32),
            pltpu.VMEM((2, 1, tq), F32), pltpu.VMEM((2, 1, tq), F32),
            pltpu.VMEM((2, 1, tq), F32),
            pltpu.VMEM((2, VT_ROWS, tq), F32),
        ],
        compiler_params=_params("parallel", "arbitrary"),
        name="diff_attn",
    )(lam, qt, k, vt, subln_g)


def _mix_kernel(x_ref, ys_ref, at_ref, sg_ref, wglu_ref, wbs_ref, wba_ref, wout_ref,
                g_ref, b_ref, o_ref):
    d = x_ref.shape[1]
    y = ys_ref[...]
    z = 0.5 * y * (1.0 + lax.erf(y * (2.0 ** -0.5)))
    gate = _sigmoid(jnp.dot(z.astype(BF16), wglu_ref[...], preferred_element_type=F32))
    glu = (z * gate).astype(BF16)
    ya = jnp.dot(glu, wbs_ref[...], preferred_element_type=F32)
    yb = jnp.dot(at_ref[...], wba_ref[...], preferred_element_type=F32)
    sg = sg_ref[...].astype(F32)
    mix = (sg[:, :d] * ya + sg[:, d:] * yb).astype(BF16)
    out = ALPHA * x_ref[...] + jnp.dot(mix, wout_ref[...], preferred_element_type=F32)
    o_ref[...] = _layer_norm(out, g_ref[...], b_ref[...])


def _resident(shape):
    return pl.BlockSpec(shape, lambda i: (0,) * len(shape), pipeline_mode=pl.Buffered(1))


def _resident_layer(w, layer):
    return _layer_block(layer, w.shape[1:], lambda i: (0, 0), pipeline_mode=pl.Buffered(1))


def _mix(x, ys, attn, sg, wglu, wbs, wba, wout, g, b, layer):
    n, d = x.shape
    tm = min(MIX_ROWS, n)
    rows = lambda w: pl.BlockSpec((tm, w), lambda i: (i, 0))
    return pl.pallas_call(
        _mix_kernel,
        grid=(n // tm,),
        in_specs=[
            rows(d), rows(ys.shape[1]), rows(attn.shape[1]), rows(sg.shape[1]),
            _resident_layer(wglu, layer), _resident_layer(wbs, layer), _resident_layer(wba, layer),
            _resident_layer(wout, layer),
            _resident((1, d)), _resident((1, d)),
        ],
        out_specs=rows(d),
        out_shape=jax.ShapeDtypeStruct((n, d), F32),
        compiler_params=_params("parallel"),
        name="mixer_out",
    )(x, ys, attn, sg, wglu, wbs, wba, wout, g, b)


def _ple_kernel(x_ref, p_ref, wg_ref, wp_ref, g_ref, b_ref, o_ref):
    x = x_ref[...]
    gate = _sigmoid(jnp.dot(x.astype(BF16), wg_ref[...], preferred_element_type=F32))
    emb = jnp.dot(p_ref[...].astype(BF16), wp_ref[...], preferred_element_type=F32)
    o_ref[...] = _layer_norm(ALPHA * x + gate * emb, g_ref[...], b_ref[...])


def _ple(x, p, wg, wp, g, b, layer):
    n, d = x.shape
    tm = min(PLE_ROWS, n)
    rows = lambda w: pl.BlockSpec((tm, w), lambda i: (i, 0))
    return pl.pallas_call(
        _ple_kernel,
        grid=(n // tm,),
        in_specs=[rows(d), pl.BlockSpec((None, None, tm, p.shape[3]), lambda i: (layer, 0, i, 0)),
                  _resident_layer(wg, layer), _resident_layer(wp, layer),
                  _resident((1, d)), _resident((1, d))],
        out_specs=rows(d),
        out_shape=jax.ShapeDtypeStruct((n, d), F32),
        compiler_params=_params("parallel"),
        name="ple_ln",
    )(x, p, wg, wp, g, b)


def _rope_tables(positions):
    half = ROT_DIM // 2
    inv_freq = ROPE_THETA ** (-jnp.arange(0, ROT_DIM, 2, dtype=F32) / ROT_DIM)
    ang = positions.astype(F32)[:, None] * inv_freq
    cos, sin = jnp.cos(ang), jnp.sin(ang)
    n = positions.shape[0]
    pad = jnp.zeros((n, HEAD_DIM - ROT_DIM), F32)
    zeros = jnp.zeros((n, half), F32)
    cos_t = jnp.concatenate([cos, cos, pad + 1.0], axis=1)
    sin_lo = jnp.concatenate([-sin, zeros, pad], axis=1)
    sin_hi = jnp.concatenate([zeros, sin, pad], axis=1)
    reps = LANES // HEAD_DIM
    return tuple(jnp.tile(t, (1, reps)) for t in (cos_t, sin_lo, sin_hi))


def kernel(x, p, positions, ffn1_w_gate, ffn1_w_up, ffn1_w_down, ln1_g, ln1_b, w_in, ssm_a_re, ssm_a_im, ssm_b_re, ssm_b_im, ssm_c_re, ssm_c_im, ssm_log_dt, ssm_d, ssm_w_glu, w_branch_ssm, lambda_q1, lambda_k1, lambda_q2, lambda_k2, attn_subln_g, w_branch_attn, w_out, ln2_g, ln2_b, ffn2_w_gate, ffn2_w_up, ffn2_w_down, ln3_g, ln3_b, ple_w_gate, ple_w_proj, ln4_g, ln4_b):
    bsz, length, d = x.shape
    assert bsz == 1
    xs = x.reshape(length, d)
    cos, slo, shi = _rope_tables(positions[0])
    tri = jnp.kron(jnp.eye(SSM_ROWS // SSM_CHUNK, dtype=F32),
                   jnp.tril(jnp.ones((SSM_CHUNK, SSM_CHUNK), F32))).astype(BF16)
    row = lambda a: a.reshape(1, -1)
    w_ffn1 = (_to_bf16(ffn1_w_gate, col_tile=FFN_COLS), _to_bf16(ffn1_w_up, col_tile=FFN_COLS),
              _to_bf16(ffn1_w_down, 0.5))
    w_ffn2 = (_to_bf16(ffn2_w_gate, col_tile=FFN_COLS), _to_bf16(ffn2_w_up, col_tile=FFN_COLS),
              _to_bf16(ffn2_w_down, 0.5))
    w_inb = _to_bf16(w_in, col_tile=PROJ_COLS)
    w_mix = (_to_bf16(ssm_w_glu), _to_bf16(w_branch_ssm), _to_bf16(w_branch_attn), _to_bf16(w_out))
    w_ple = (_to_bf16(ple_w_gate), _to_bf16(ple_w_proj))
    tables = jax.vmap(_ssm_tables)(ssm_a_re, ssm_a_im, ssm_b_re, ssm_b_im, ssm_c_re, ssm_c_im, ssm_log_dt)
    for i in range(DEPTH):
        lam_init = 0.8 - 0.6 * math.exp(-0.3 * i)
        lam = (jnp.exp(jnp.sum(lambda_q1[i] * lambda_k1[i])) - jnp.exp(jnp.sum(lambda_q2[i] * lambda_k2[i]))
               + lam_init).reshape(1, 1)
        xs, xb = _ffn_ln(xs, *w_ffn1, row(ln1_g[i]), row(ln1_b[i]), layer=i, emit_bf16=True)
        u, qt, k, vt, sg = _inproj(xb, w_inb, cos, slo, shi, layer=i)
        ys = _ssm(u, tables, row(ssm_d[i]), tri, layer=i)
        attn = _attention(qt, k, vt, lam, row(attn_subln_g[i]), lam_init)
        xs = _mix(xs, ys, attn, sg, *w_mix, row(ln2_g[i]), row(ln2_b[i]), layer=i)
        xs = _ffn_ln(xs, *w_ffn2, row(ln3_g[i]), row(ln3_b[i]), layer=i, emit_bf16=False)
        xs = _ple(xs, p, *w_ple, row(ln4_g[i]), row(ln4_b[i]), layer=i)
    return xs.reshape(bsz, length, d)
```
